```python
import jax, jax.numpy as jnp
from jax import lax
import numpy as np

D_MODEL = 1024
BATCH = 2
SEQ = 8192
DEPTH = 2

CHUNK = 64
N_MIXERS = 2
FOX_HEADS = 8
FOX_HEAD_DIM = D_MODEL // FOX_HEADS
Q_BLOCK = 128
RET_HEADS = 4
RET_QK_DIM = D_MODEL // RET_HEADS
RET_V_DIM = 2 * D_MODEL // RET_HEADS
D_FF = -(-8 * D_MODEL // (3 * 256)) * 256
ROPE_BASE = 10000.0
EPS = 1e-6
N_FOX_LAYERS = (DEPTH + 1) // 2
N_RET_LAYERS = DEPTH // 2

kernel_name = "fox_retnet_interleaved_trunk"


def rmsnorm(x, g):
    xf = x.astype(jnp.float32)
    y = xf * lax.rsqrt(jnp.mean(xf * xf, axis=-1, keepdims=True) + EPS)
    return (y * g.astype(jnp.float32)).astype(x.dtype)


def forgetting_attention(h, w_in, b_f, w_out):
    B, S, _ = h.shape
    H, dh = FOX_HEADS, FOX_HEAD_DIM
    proj = h @ w_in
    q = proj[..., :D_MODEL].reshape(B, S, H, dh).transpose(0, 2, 1, 3)
    k = proj[..., D_MODEL:2 * D_MODEL].reshape(B, S, H, dh).transpose(0, 2, 1, 3)
    v = proj[..., 2 * D_MODEL:3 * D_MODEL].reshape(B, S, H, dh).transpose(0, 2, 1, 3)
    log_f = jax.nn.log_sigmoid((proj[..., 3 * D_MODEL:] + b_f).astype(jnp.float32))
    c = jnp.cumsum(log_f, axis=1).transpose(0, 2, 1)
    nb = S // Q_BLOCK
    qb = q.reshape(B, H, nb, Q_BLOCK, dh).transpose(2, 0, 1, 3, 4)
    cb = c.reshape(B, H, nb, Q_BLOCK).transpose(2, 0, 1, 3)
    pos = jnp.arange(S)
    qpos = pos.reshape(nb, Q_BLOCK)
    scale = dh ** -0.5

    def block(args):
        qi, ci, pi = args
        s = jnp.einsum('bhqd,bhkd->bhqk', qi, k).astype(jnp.float32) * scale
        s = s + (ci[..., :, None] - c[:, :, None, :])
        s = jnp.where(pi[:, None] >= pos[None, :], s, -jnp.inf)
        p = jax.nn.softmax(s, axis=-1).astype(v.dtype)
        return jnp.einsum('bhqk,bhkd->bhqd', p, v)

    o = lax.map(block, (qb, cb, qpos))
    o = o.transpose(1, 0, 3, 2, 4).reshape(B, S, H * dh)
    return o @ w_out


def rotate(t, cos, sin):
    half = t.shape[-1] // 2
    t1, t2 = t[..., :half], t[..., half:]
    return jnp.concatenate([t1 * cos - t2 * sin, t1 * sin + t2 * cos], axis=-1)


def retention(h, w_in, w_out):
    B, S, _ = h.shape
    H, dk, dv, C = RET_HEADS, RET_QK_DIM, RET_V_DIM, CHUNK
    N = S // C
    proj = h @ w_in
    q = proj[..., :D_MODEL].reshape(B, S, H, dk)
    k = proj[..., D_MODEL:2 * D_MODEL].reshape(B, S, H, dk)
    v = proj[..., 2 * D_MODEL:4 * D_MODEL].reshape(B, S, H, dv)
    g = proj[..., 4 * D_MODEL:]
    half = dk // 2
    inv = ROPE_BASE ** (-jnp.arange(half, dtype=jnp.float32) / half)
    ang = jnp.arange(S, dtype=jnp.float32)[:, None] * inv[None, :]
    cos, sin = jnp.cos(ang)[:, None, :], jnp.sin(ang)[:, None, :]
    q = rotate(q.astype(jnp.float32), cos, sin)
    k = rotate(k.astype(jnp.float32), cos, sin) * (dk ** -0.5)
    v = v.astype(jnp.float32)
    log_gamma = jnp.log(jnp.asarray(1.0 - 2.0 ** (-5.0 - np.arange(H)), dtype=jnp.float32))
    idx = jnp.arange(C, dtype=jnp.float32)
    d_intra = jnp.exp(log_gamma[:, None, None] * jnp.abs(idx[:, None] - idx[None, :]))
    q_decay = jnp.exp(log_gamma[:, None] * (idx[None, :] + 1.0))
    k_decay = jnp.exp(log_gamma[:, None] * (C - 1.0 - idx[None, :]))
    chunk_decay = jnp.exp(log_gamma * C)
    to_chunks = lambda t: t.reshape(B, N, C, H, t.shape[-1]).transpose(1, 0, 3, 2, 4)
    qc, kc, vc = to_chunks(q), to_chunks(k), to_chunks(v)

    def step(state, inp):
        qi, ki, vi = inp
        a = jnp.einsum('bhid,bhjd->bhij', qi, ki) * d_intra
        inner = jnp.einsum('bhij,bhjv->bhiv', a, vi)
        cross = jnp.einsum('bhid,bhdv->bhiv', qi, state) * q_decay[None, :, :, None]
        new_state = state * chunk_decay[None, :, None, None] + jnp.einsum(
            'bhjd,bhjv->bhdv', ki * k_decay[None, :, :, None], vi)
        return new_state, inner + cross

    state0 = jnp.zeros((B, H, dk, dv), jnp.float32)
    _, y = lax.scan(step, state0, (qc, kc, vc))
    y = y.transpose(1, 0, 3, 2, 4).reshape(B, S, H, dv)
    mu = jnp.mean(y, axis=-1, keepdims=True)
    var = jnp.mean(jnp.square(y - mu), axis=-1, keepdims=True)
    y = ((y - mu) * lax.rsqrt(var + EPS)).reshape(B, S, H * dv).astype(h.dtype)
    return (jax.nn.silu(g) * y) @ w_out


def swiglu(h, w_in, w_out):
    gu = h @ w_in
    return (jax.nn.silu(gu[..., :D_FF]) * gu[..., D_FF:]) @ w_out


def setup_inputs(seed: int = 0) -> dict:
    key = jax.random.key(seed)
    ks = jax.random.split(key, 12)
    f32 = jnp.float32
    x = jax.random.normal(ks[0], (BATCH, SEQ, D_MODEL), f32)
    norm_mix = 1.0 + 0.02 * jax.random.normal(ks[1], (DEPTH, D_MODEL), f32)
    norm_ffn = 1.0 + 0.02 * jax.random.normal(ks[2], (DEPTH, D_MODEL), f32)
    fox_w_in = jax.random.normal(ks[3], (N_FOX_LAYERS, D_MODEL, 3 * D_MODEL + FOX_HEADS), f32) * D_MODEL ** -0.5
    fox_b_f = 3.0 + 0.5 * jax.random.normal(ks[4], (N_FOX_LAYERS, FOX_HEADS), f32)
    fox_w_out = jax.random.normal(ks[5], (N_FOX_LAYERS, D_MODEL, D_MODEL), f32) * D_MODEL ** -0.5
    ret_w_in = jax.random.normal(ks[6], (N_RET_LAYERS, D_MODEL, 6 * D_MODEL), f32) * D_MODEL ** -0.5
    ret_w_out = jax.random.normal(ks[7], (N_RET_LAYERS, 2 * D_MODEL, D_MODEL), f32) * (2 * D_MODEL) ** -0.5
    ffn_w_in = jax.random.normal(ks[8], (DEPTH, D_MODEL, 2 * D_FF), f32) * D_MODEL ** -0.5
    ffn_w_out = jax.random.normal(ks[9], (DEPTH, D_FF, D_MODEL), f32) * D_FF ** -0.5
    final_norm = 1.0 + 0.02 * jax.random.normal(ks[10], (D_MODEL,), f32)
    return {"x": x, "norm_mix": norm_mix, "norm_ffn": norm_ffn,
            "fox_w_in": fox_w_in, "fox_b_f": fox_b_f, "fox_w_out": fox_w_out,
            "ret_w_in": ret_w_in, "ret_w_out": ret_w_out,
            "ffn_w_in": ffn_w_in, "ffn_w_out": ffn_w_out, "final_norm": final_norm}


def reference(x, norm_mix, norm_ffn, fox_w_in, fox_b_f, fox_w_out, ret_w_in, ret_w_out,
              ffn_w_in, ffn_w_out, final_norm):
    for i in range(DEPTH):
        h = rmsnorm(x, norm_mix[i])
        j = i // N_MIXERS
        if i % N_MIXERS == 0:
            x = x + forgetting_attention(h, fox_w_in[j], fox_b_f[j], fox_w_out[j])
        else:
            x = x + retention(h, ret_w_in[j], ret_w_out[j])
        h = rmsnorm(x, norm_ffn[i])
        x = x + swiglu(h, ffn_w_in[i], ffn_w_out[i])
    return rmsnorm(x, final_norm)
```

```python
import functools
import math

import jax
import jax.numpy as jnp
import numpy as np
from jax import lax
from jax.experimental import pallas as pl
from jax.experimental.pallas import tpu as pltpu

D_MODEL = 1024
CHUNK = 64
FOX_HEADS = 8
FOX_HEAD_DIM = D_MODEL // FOX_HEADS
RET_HEADS = 4
RET_QK_DIM = D_MODEL // RET_HEADS
RET_V_DIM = 2 * D_MODEL // RET_HEADS
D_FF = -(-8 * D_MODEL // (3 * 256)) * 256
ROPE_BASE = 10000.0
EPS = 1e-6
LOG2E = math.log2(math.e)

F32 = jnp.float32
BF16 = jnp.bfloat16

VMEM_LIMIT_BYTES = 56 * 1024 * 1024

ROW_TILE = 512
ATTN_TILE = 512
RET_BLOCK = 256
FFN_CHUNKS = ((0, 1024), (1024, 1024), (2048, 768))


def _params(*semantics):
    return pltpu.CompilerParams(dimension_semantics=semantics, vmem_limit_bytes=VMEM_LIMIT_BYTES)


def _rmsnorm(x, g):
    return x * lax.rsqrt(jnp.mean(x * x, axis=-1, keepdims=True) + EPS) * g


def _dot(a, b):
    return jnp.dot(a, b, preferred_element_type=F32)


def _dot_nt(a, b):
    return lax.dot_general(a, b, (((1,), (1,)), ((), ())), preferred_element_type=F32)


def _dot_tn(a, b):
    return lax.dot_general(a, b, (((0,), (0,)), ((), ())), preferred_element_type=F32)


def _cumsum_lanes(v):
    n = v.shape[-1]
    lane = lax.broadcasted_iota(jnp.int32, v.shape, 1)
    shift = 1
    while shift < n:
        rolled = pltpu.roll(v, shift, axis=1)
        v = v + jnp.where(lane >= shift, rolled, 0.0)
        shift *= 2
    return v


def _fox_proj_kernel(x_ref, g_ref, w_ref, wf_ref, bf_ref, qkv_ref, c_ref, carry_ref, *, tiles_per_seq):
    i = pl.program_id(0)
    tm = x_ref.shape[0]
    h = _rmsnorm(x_ref[...], g_ref[...]).astype(BF16)
    q_scale = FOX_HEAD_DIM ** -0.5 * LOG2E
    tn = 512
    for n0 in range(0, 3 * D_MODEL, tn):
        acc = _dot(h, w_ref[:, n0:n0 + tn])
        if n0 < D_MODEL:
            acc = acc * q_scale
        qkv_ref[:, n0:n0 + tn] = acc.astype(BF16)

    logits = _dot_nt(wf_ref[...], h)[:FOX_HEADS] + bf_ref[...]
    log_f = jnp.minimum(logits, 0.0) - jnp.log1p(jnp.exp(-jnp.abs(logits)))

    @pl.when(i % tiles_per_seq == 0)
    def _():
        carry_ref[...] = jnp.zeros_like(carry_ref)

    c = _cumsum_lanes(log_f) + carry_ref[:, :1]
    carry_ref[...] = jnp.broadcast_to(c[:, tm - 1:tm], carry_ref.shape)
    c_ref[...] = c * (-LOG2E)


def _fox_proj(x2d, g, w_qkv, wf_t, bf, seq):
    t = x2d.shape[0]
    tm = ROW_TILE
    return pl.pallas_call(
        functools.partial(_fox_proj_kernel, tiles_per_seq=seq // tm),
        grid=(t // tm,),
        in_specs=[
            pl.BlockSpec((tm, D_MODEL), lambda i: (i, 0)),
            pl.BlockSpec((1, D_MODEL), lambda i: (0, 0)),
            pl.BlockSpec((D_MODEL, 3 * D_MODEL), lambda i: (0, 0)),
            pl.BlockSpec((16, D_MODEL), lambda i: (0, 0)),
            pl.BlockSpec((FOX_HEADS, 1), lambda i: (0, 0)),
        ],
        out_specs=[
            pl.BlockSpec((tm, 3 * D_MODEL), lambda i: (i, 0)),
            pl.BlockSpec((FOX_HEADS, tm), lambda i: (0, i)),
        ],
        out_shape=[
            jax.ShapeDtypeStruct((t, 3 * D_MODEL), BF16),
            jax.ShapeDtypeStruct((FOX_HEADS, t), F32),
        ],
        scratch_shapes=[pltpu.VMEM((FOX_HEADS, 128), F32)],
        compiler_params=_params("arbitrary"),
        name="fox_proj",
    )(x2d, g, w_qkv, wf_t, bf)


def _fox_attn_kernel(q_ref, k_ref, v_ref, c_ref, o_ref):
    i = pl.program_id(2)
    tq = q_ref.shape[0]
    tk = tq
    q = q_ref[...]

    def step(j, carry, masked):
        m, l, acc = carry
        start = pl.multiple_of(j * tk, tk)
        k = k_ref[pl.ds(start, tk), :]
        v = v_ref[pl.ds(start, tk), :]
        s = _dot_nt(q, k) + c_ref[j]
        if masked:
            row = lax.broadcasted_iota(jnp.int32, s.shape, 0)
            col = lax.broadcasted_iota(jnp.int32, s.shape, 1)
            s = jnp.where(row >= col, s, -jnp.inf)
        m_new = jnp.maximum(m, jnp.max(s, axis=-1, keepdims=True))
        alpha = jnp.exp2(m - m_new)
        p = jnp.exp2(s - m_new)
        l = alpha * l + jnp.sum(p, axis=-1, keepdims=True)
        acc = alpha * acc + _dot(p.astype(BF16), v)
        return m_new, l, acc

    init = (jnp.full((tq, 1), -jnp.inf, F32), jnp.zeros((tq, 1), F32),
            jnp.zeros((tq, FOX_HEAD_DIM), F32))
    carry = lax.fori_loop(0, i, lambda j, c: step(j, c, False), init)
    _, l, acc = step(i, carry, True)
    o_ref[...] = (acc / l).astype(o_ref.dtype)


def _fox_attn(qkv, c_neg, batch, seq):
    t = qkv.shape[0]
    tq = ATTN_TILE
    nq = seq // tq
    hb = D_MODEL // FOX_HEAD_DIM
    c4 = c_neg.reshape(FOX_HEADS, t // tq, 1, tq)
    return pl.pallas_call(
        _fox_attn_kernel,
        grid=(batch, FOX_HEADS, nq),
        in_specs=[
            pl.BlockSpec((tq, FOX_HEAD_DIM), lambda b, h, i: (b * nq + i, h)),
            pl.BlockSpec((seq, FOX_HEAD_DIM), lambda b, h, i: (b, hb + h)),
            pl.BlockSpec((seq, FOX_HEAD_DIM), lambda b, h, i: (b, 2 * hb + h)),
            pl.BlockSpec((None, nq, 1, tq), lambda b, h, i: (h, b, 0, 0)),
        ],
        out_specs=pl.BlockSpec((tq, FOX_HEAD_DIM), lambda b, h, i: (b * nq + i, h)),
        out_shape=jax.ShapeDtypeStruct((t, D_MODEL), BF16),
        compiler_params=_params("arbitrary", "arbitrary", "arbitrary"),
        name="fox_attn",
    )(qkv, qkv, qkv, c4)


def _out_proj_kernel(a_ref, w_ref, r_ref, o_ref):
    o_ref[...] = r_ref[...] + _dot(a_ref[...], w_ref[...])


def _out_proj(a, w, resid, name):
    t, k = a.shape
    tm = ROW_TILE
    return pl.pallas_call(
        _out_proj_kernel,
        grid=(t // tm,),
        in_specs=[
            pl.BlockSpec((tm, k), lambda i: (i, 0)),
            pl.BlockSpec((k, D_MODEL), lambda i: (0, 0)),
            pl.BlockSpec((tm, D_MODEL), lambda i: (i, 0)),
        ],
        out_specs=pl.BlockSpec((tm, D_MODEL), lambda i: (i, 0)),
        out_shape=jax.ShapeDtypeStruct((t, D_MODEL), F32),
        compiler_params=_params("arbitrary"),
        name=name,
    )(a, w, resid)


def _ffn_kernel(x_ref, g_ref, wg_ref, wu_ref, wo_ref, fg_ref, o_ref, *, final_norm):
    x = x_ref[...]
    h = _rmsnorm(x, g_ref[...]).astype(BF16)
    out = x
    for c0, width in FFN_CHUNKS:
        gate = _dot(h, wg_ref[:, c0:c0 + width])
        up = _dot(h, wu_ref[:, c0:c0 + width])
        act = (gate * jax.nn.sigmoid(gate) * up).astype(BF16)
        out = out + _dot(act, wo_ref[c0:c0 + width, :])
    if final_norm:
        out = _rmsnorm(out, fg_ref[...])
    o_ref[...] = out


def _ffn(x2d, g, w_gate, w_up, w_out, final_g, final_norm, name):
    t = x2d.shape[0]
    tm = ROW_TILE
    return pl.pallas_call(
        functools.partial(_ffn_kernel, final_norm=final_norm),
        grid=(t // tm,),
        in_specs=[
            pl.BlockSpec((tm, D_MODEL), lambda i: (i, 0)),
            pl.BlockSpec((1, D_MODEL), lambda i: (0, 0)),
            pl.BlockSpec((D_MODEL, D_FF), lambda i: (0, 0)),
            pl.BlockSpec((D_MODEL, D_FF), lambda i: (0, 0)),
            pl.BlockSpec((D_FF, D_MODEL), lambda i: (0, 0)),
            pl.BlockSpec((1, D_MODEL), lambda i: (0, 0)),
        ],
        out_specs=pl.BlockSpec((tm, D_MODEL), lambda i: (i, 0)),
        out_shape=jax.ShapeDtypeStruct((t, D_MODEL), F32),
        compiler_params=_params("arbitrary"),
        name=name,
    )(x2d, g, w_gate, w_up, w_out, final_g)


def _ret_proj_kernel(x_ref, g_ref, w_ref, cos_ref, sin_ref, q_ref, k_ref, v_ref, sg_ref):
    h = _rmsnorm(x_ref[...], g_ref[...]).astype(BF16)
    cos = cos_ref[...]
    sin = sin_ref[...]
    half = RET_QK_DIM // 2
    tn = 512

    def rotated(out_ref, base, scale):
        for n0 in range(0, D_MODEL, tn):
            acc = _dot(h, w_ref[:, base + n0:base + n0 + tn])
            for d0 in range(0, tn, RET_QK_DIM):
                t1 = acc[:, d0:d0 + half]
                t2 = acc[:, d0 + half:d0 + RET_QK_DIM]
                out_ref[:, n0 + d0:n0 + d0 + half] = ((t1 * cos - t2 * sin) * scale).astype(BF16)
                out_ref[:, n0 + d0 + half:n0 + d0 + RET_QK_DIM] = ((t1 * sin + t2 * cos) * scale).astype(BF16)

    rotated(q_ref, 0, 1.0)
    rotated(k_ref, D_MODEL, RET_QK_DIM ** -0.5)
    for n0 in range(0, 2 * D_MODEL, tn):
        v_ref[:, n0:n0 + tn] = _dot(h, w_ref[:, 2 * D_MODEL + n0:2 * D_MODEL + n0 + tn]).astype(BF16)
    for n0 in range(0, 2 * D_MODEL, tn):
        gate = _dot(h, w_ref[:, 4 * D_MODEL + n0:4 * D_MODEL + n0 + tn])
        sg_ref[:, n0:n0 + tn] = (gate * jax.nn.sigmoid(gate)).astype(BF16)


def _ret_proj(x2d, g, w, cos, sin, seq):
    t = x2d.shape[0]
    tm = ROW_TILE
    tiles_per_seq = seq // tm
    half = RET_QK_DIM // 2
    row = lambda i: (i, 0)
    return pl.pallas_call(
        _ret_proj_kernel,
        grid=(t // tm,),
        in_specs=[
            pl.BlockSpec((tm, D_MODEL), row),
            pl.BlockSpec((1, D_MODEL), lambda i: (0, 0)),
            pl.BlockSpec((D_MODEL, 6 * D_MODEL), lambda i: (0, 0)),
            pl.BlockSpec((tm, half), lambda i: (i % tiles_per_seq, 0)),
            pl.BlockSpec((tm, half), lambda i: (i % tiles_per_seq, 0)),
        ],
        out_specs=[
            pl.BlockSpec((tm, D_MODEL), row),
            pl.BlockSpec((tm, D_MODEL), row),
            pl.BlockSpec((tm, 2 * D_MODEL), row),
            pl.BlockSpec((tm, 2 * D_MODEL), row),
        ],
        out_shape=[
            jax.ShapeDtypeStruct((t, D_MODEL), BF16),
            jax.ShapeDtypeStruct((t, D_MODEL), BF16),
            jax.ShapeDtypeStruct((t, 2 * D_MODEL), BF16),
            jax.ShapeDtypeStruct((t, 2 * D_MODEL), BF16),
        ],
        compiler_params=_params("arbitrary"),
        name="ret_proj",
    )(x2d, g, w, cos, sin)


def _retention_kernel(q_ref, k_ref, v_ref, sg_ref, w_ref, qd_ref, kd_ref, o_ref, state_ref, *, block_decay):
    n = pl.program_id(2)
    hd = pl.program_id(1)

    @pl.when(n == 0)
    def _():
        state_ref[...] = jnp.zeros_like(state_ref)

    q = q_ref[...]
    k = k_ref[...]
    v = v_ref[...]
    state = state_ref[...]
    scores = (_dot_nt(q, k) * w_ref[...]).astype(BF16)
    y = _dot(scores, v) + _dot(q, state.astype(BF16)) * qd_ref[...]
    k_dec = (k.astype(F32) * kd_ref[...]).astype(BF16)
    decay = jnp.float32(block_decay[0])
    for idx in range(1, RET_HEADS):
        decay = jnp.where(hd == idx, jnp.float32(block_decay[idx]), decay)
    state_ref[...] = state * decay + _dot_tn(k_dec, v)

    mu = jnp.mean(y, axis=-1, keepdims=True)
    yc = y - mu
    var = jnp.mean(yc * yc, axis=-1, keepdims=True)
    o_ref[...] = (sg_ref[...].astype(F32) * (yc * lax.rsqrt(var + EPS))).astype(o_ref.dtype)


def _retention_tables():
    r = RET_BLOCK
    gamma = 1.0 - 2.0 ** (-5.0 - np.arange(RET_HEADS, dtype=np.float64))
    log_gamma = jnp.log(jnp.asarray(gamma, dtype=F32))
    pos = np.arange(r)
    diff = (pos[:, None] - pos[None, :]).astype(np.float32)
    same = (pos[:, None] // CHUNK) == (pos[None, :] // CHUNK)
    earlier = (pos[None, :] // CHUNK) < (pos[:, None] // CHUNK)
    expo = np.where(same, np.abs(diff), diff)
    w = jnp.where(jnp.asarray(same | earlier)[None],
                  jnp.exp(log_gamma[:, None, None] * jnp.asarray(expo)[None]), 0.0)
    idx = jnp.arange(r, dtype=F32)
    qd = jnp.exp(log_gamma[:, None] * (idx[None, :] + 1.0))[..., None]
    kd = jnp.exp(log_gamma[:, None] * (r - 1.0 - idx[None, :]))[..., None]
    block_decay = tuple(float(g) ** r for g in gamma.astype(np.float32).astype(np.float64))
    return w, qd, kd, block_decay


def _retention(q, k, v, sg, batch, seq):
    t = q.shape[0]
    r = RET_BLOCK
    nb = seq // r
    w, qd, kd, block_decay = _retention_tables()
    blk = lambda b, h, n: (b * nb + n, h)
    per_head = lambda b, h, n: (h, 0, 0)
    return pl.pallas_call(
        functools.partial(_retention_kernel, block_decay=block_decay),
        grid=(batch, RET_HEADS, nb),
        in_specs=[
            pl.BlockSpec((r, RET_QK_DIM), blk),
            pl.BlockSpec((r, RET_QK_DIM), blk),
            pl.BlockSpec((r, RET_V_DIM), blk),
            pl.BlockSpec((r, RET_V_DIM), blk),
            pl.BlockSpec((None, r, r), per_head),
            pl.BlockSpec((None, r, 1), per_head),
            pl.BlockSpec((None, r, 1), per_head),
        ],
        out_specs=pl.BlockSpec((r, RET_V_DIM), blk),
        out_shape=jax.ShapeDtypeStruct((t, 2 * D_MODEL), BF16),
        scratch_shapes=[pltpu.VMEM((RET_QK_DIM, RET_V_DIM), F32)],
        compiler_params=_params("arbitrary", "arbitrary", "arbitrary"),
        name="retention",
    )(q, k, v, sg, w, qd, kd)


def kernel(x, norm_mix, norm_ffn, fox_w_in, fox_b_f, fox_w_out, ret_w_in, ret_w_out,
           ffn_w_in, ffn_w_out, final_norm):
    batch, seq, d = x.shape
    t = batch * seq
    x2d = x.reshape(t, d)
    row = lambda a: a.reshape(1, -1)

    w_in = fox_w_in[0]
    w_qkv = w_in[:, :3 * D_MODEL].astype(BF16)
    wf_t = jnp.zeros((16, D_MODEL), BF16).at[:FOX_HEADS].set(w_in[:, 3 * D_MODEL:].T.astype(BF16))
    qkv, c_neg = _fox_proj(x2d, row(norm_mix[0]), w_qkv, wf_t, fox_b_f[0].reshape(FOX_HEADS, 1), seq)
    attn = _fox_attn(qkv, c_neg, batch, seq)
    x2d = _out_proj(attn, fox_w_out[0].astype(BF16), x2d, "fox_out")
    x2d = _ffn(x2d, row(norm_ffn[0]), ffn_w_in[0][:, :D_FF].astype(BF16), ffn_w_in[0][:, D_FF:].astype(BF16),
               ffn_w_out[0].astype(BF16), row(final_norm), False, "ffn0")

    half = RET_QK_DIM // 2
    inv = ROPE_BASE ** (-jnp.arange(half, dtype=F32) / half)
    ang = jnp.arange(seq, dtype=F32)[:, None] * inv[None, :]
    q, k, v, sg = _ret_proj(x2d, row(norm_mix[1]), ret_w_in[0].astype(BF16), jnp.cos(ang), jnp.sin(ang), seq)
    gated = _retention(q, k, v, sg, batch, seq)
    x2d = _out_proj(gated, ret_w_out[0].astype(BF16), x2d, "ret_out")
    x2d = _ffn(x2d, row(norm_ffn[1]), ffn_w_in[1][:, :D_FF].astype(BF16), ffn_w_in[1][:, D_FF:].astype(BF16),
               ffn_w_out[1].astype(BF16), row(final_norm), True, "ffn1")
    return x2d.reshape(batch, seq, d)
```

```python
import functools
import math

import jax
import jax.numpy as jnp
import numpy as np
from jax import lax
from jax.experimental import pallas as pl
from jax.experimental.pallas import tpu as pltpu

D_MODEL = 1024
CHUNK = 64
FOX_HEADS = 8
FOX_HEAD_DIM = D_MODEL // FOX_HEADS
RET_HEADS = 4
RET_QK_DIM = D_MODEL // RET_HEADS
RET_V_DIM = 2 * D_MODEL // RET_HEADS
D_FF = -(-8 * D_MODEL // (3 * 256)) * 256
ROPE_BASE = 10000.0
EPS = 1e-6
LOG2E = math.log2(math.e)

F32 = jnp.float32
BF16 = jnp.bfloat16

VMEM_LIMIT_BYTES = 56 * 1024 * 1024
LANES = 128

ROW_TILE = 512
ATTN_Q_TILE = 2 * ROW_TILE
ATTN_K_TILE = ATTN_Q_TILE // 4
RET_BLOCK = 256
FFN_CHUNKS = ((0, 1024), (1024, 1024), (2048, 768))
C_PARTS = 3
ONES_ROWS = 16


def _params(*semantics):
    return pltpu.CompilerParams(dimension_semantics=semantics, vmem_limit_bytes=VMEM_LIMIT_BYTES)


def _rmsnorm(x, g):
    return x * lax.rsqrt(jnp.mean(x * x, axis=-1, keepdims=True) + EPS) * g


def _dot(a, b):
    return jnp.dot(a, b, preferred_element_type=F32)


def _dot_nt(a, b):
    return lax.dot_general(a, b, (((1,), (1,)), ((), ())), preferred_element_type=F32)


def _dot_tn(a, b):
    return lax.dot_general(a, b, (((0,), (0,)), ((), ())), preferred_element_type=F32)


def _cumsum_rows(v):
    n = v.shape[0]
    row = lax.broadcasted_iota(jnp.int32, v.shape, 0)
    shift = 1
    while shift < n:
        rolled = pltpu.roll(v, shift, axis=0)
        v = v + jnp.where(row >= shift, rolled, 0.0)
        shift *= 2
    return v


def _fox_proj_kernel(x_ref, g_ref, wqk_ref, wvt_ref, wf_ref, bf_ref, qk_ref, vt_ref, ca_ref, carry_ref,
                     *, tiles_per_seq):
    i = pl.program_id(0)
    tm = x_ref.shape[0]
    h = _rmsnorm(x_ref[...], g_ref[...]).astype(BF16)
    q_scale = FOX_HEAD_DIM ** -0.5 * LOG2E
    tn = 512
    for n0 in range(0, 2 * D_MODEL, tn):
        acc = _dot(h, wqk_ref[:, n0:n0 + tn])
        if n0 < D_MODEL:
            acc = acc * q_scale
        qk_ref[:, n0:n0 + tn] = acc.astype(BF16)
    for r0 in range(0, D_MODEL, tn):
        vt_ref[r0:r0 + tn, :] = _dot_nt(wvt_ref[r0:r0 + tn, :], h).astype(BF16)

    logits = _dot(h, wf_ref[...]) + bf_ref[...]
    log_f = jnp.minimum(logits, 0.0) - jnp.log1p(jnp.exp(-jnp.abs(logits)))

    @pl.when(i % tiles_per_seq == 0)
    def _():
        carry_ref[...] = jnp.zeros_like(carry_ref)

    c = _cumsum_rows(log_f) + carry_ref[...]
    carry_ref[...] = c[tm - 1:tm, :]
    neg_c = c * (-LOG2E)
    hi = neg_c.astype(BF16)
    rest = neg_c - hi.astype(F32)
    mid = rest.astype(BF16)
    lo = (rest - mid.astype(F32)).astype(BF16)
    lane = lax.broadcasted_iota(jnp.int32, neg_c.shape, 1)
    ca_ref[...] = jnp.where(lane < FOX_HEADS, hi, jnp.where(lane < 2 * FOX_HEADS, mid, lo))


def _fox_proj(x2d, g, w_qk, w_vt, wf, bf, seq):
    t = x2d.shape[0]
    tm = ROW_TILE
    return pl.pallas_call(
        functools.partial(_fox_proj_kernel, tiles_per_seq=seq // tm),
        grid=(t // tm,),
        in_specs=[
            pl.BlockSpec((tm, D_MODEL), lambda i: (i, 0)),
            pl.BlockSpec((1, D_MODEL), lambda i: (0, 0)),
            pl.BlockSpec((D_MODEL, 2 * D_MODEL), lambda i: (0, 0)),
            pl.BlockSpec((D_MODEL, D_MODEL), lambda i: (0, 0)),
            pl.BlockSpec((D_MODEL, LANES), lambda i: (0, 0)),
            pl.BlockSpec((1, LANES), lambda i: (0, 0)),
        ],
        out_specs=[
            pl.BlockSpec((tm, 2 * D_MODEL), lambda i: (i, 0)),
            pl.BlockSpec((None, D_MODEL, tm), lambda i: (i, 0, 0)),
            pl.BlockSpec((tm, LANES), lambda i: (i, 0)),
        ],
        out_shape=[
            jax.ShapeDtypeStruct((t, 2 * D_MODEL), BF16),
            jax.ShapeDtypeStruct((t // tm, D_MODEL, tm), BF16),
            jax.ShapeDtypeStruct((t, LANES), BF16),
        ],
        scratch_shapes=[pltpu.VMEM((1, LANES), F32)],
        compiler_params=_params("arbitrary"),
        name="fox_proj",
    )(x2d, g, w_qk, w_vt, wf, bf)


def _fox_attn_kernel(q_ref, k_ref, ca_ref, vt_ref, o_ref, s_ref, m_ref, acc_ref):
    head = pl.program_id(1)
    i = pl.program_id(2)
    tq = q_ref.shape[0]
    tk = ATTN_K_TILE
    tv = vt_ref.shape[2]
    assert tq == 4 * tk and tq == 2 * tv

    lane = lax.broadcasted_iota(jnp.int32, (tq, LANES), 1)
    pick = jnp.logical_and(lane % FOX_HEADS == head, lane < C_PARTS * FOX_HEADS)
    q_ext = jnp.concatenate([q_ref[...], jnp.where(pick, 1.0, 0.0).astype(BF16)], axis=1)

    def scores(row0, c0=0):
        k_ext = jnp.concatenate([k_ref[pl.ds(row0, tk), :], ca_ref[pl.ds(row0, tk), :]], axis=1)
        return _dot_nt(k_ext, q_ext[c0:, :])

    def update(tiles, vt, c0=0, diagonal=False):
        if diagonal:
            (s,) = tiles
            key = lax.broadcasted_iota(jnp.int32, s.shape, 0)
            qry = lax.broadcasted_iota(jnp.int32, s.shape, 1)
            tiles = [jnp.where(qry >= key, s, -jnp.inf)]
        m_old = m_ref[:, c0:]
        m_new = m_old
        for s in tiles:
            m_new = jnp.maximum(m_new, jnp.max(s, axis=0, keepdims=True))
        alpha = jnp.exp2(m_old - m_new)
        p = [jnp.exp2(s - m_new).astype(BF16) for s in tiles]
        p = p[0] if len(p) == 1 else jnp.concatenate(p, axis=0)
        vt_ext = jnp.concatenate([vt, jnp.ones((ONES_ROWS, vt.shape[1]), BF16)], axis=0)
        acc_ref[:, c0:] = alpha * acc_ref[:, c0:] + _dot(vt_ext, p)
        m_ref[:, c0:] = m_new

    m_ref[...] = jnp.full_like(m_ref, -jnp.inf)
    acc_ref[...] = jnp.zeros_like(acc_ref)
    s_ref[0] = scores(0)
    s_ref[1] = scores(tk)

    def body(j, carry):
        base = pl.multiple_of(j * tq, tq)
        s_ref[2] = scores(base + 2 * tk)
        s_ref[3] = scores(base + 3 * tk)
        update([s_ref[0], s_ref[1]], vt_ref[2 * j])
        s_ref[0] = scores(base + tq)
        s_ref[1] = scores(base + tq + tk)
        update([s_ref[2], s_ref[3]], vt_ref[2 * j + 1])
        return carry

    lax.fori_loop(0, i, body, 0)
    base = pl.multiple_of(i * tq, tq)
    s2 = scores(base + 2 * tk, 2 * tk)
    s3 = scores(base + 3 * tk, 3 * tk)
    update([s_ref[0]], vt_ref[2 * i, :, 0:tk], 0, True)
    update([s_ref[1, :, tk:]], vt_ref[2 * i, :, tk:tv], tk, True)
    update([s2], vt_ref[2 * i + 1, :, 0:tk], 2 * tk, True)
    update([s3], vt_ref[2 * i + 1, :, tk:tv], 3 * tk, True)
    acc = acc_ref[...]
    out = acc[:FOX_HEAD_DIM] / acc[FOX_HEAD_DIM:FOX_HEAD_DIM + 1]
    o_ref[...] = jnp.transpose(out).astype(o_ref.dtype)


def _fox_attn(qk, ca, vt, batch, seq):
    t = qk.shape[0]
    tq = ATTN_Q_TILE
    tk = ATTN_K_TILE
    nq = seq // tq
    tv = vt.shape[2]
    hb = D_MODEL // FOX_HEAD_DIM
    return pl.pallas_call(
        _fox_attn_kernel,
        grid=(batch, FOX_HEADS, nq),
        in_specs=[
            pl.BlockSpec((tq, FOX_HEAD_DIM), lambda b, h, i: (b * nq + i, h)),
            pl.BlockSpec((seq, FOX_HEAD_DIM), lambda b, h, i: (b, hb + h)),
            pl.BlockSpec((seq, LANES), lambda b, h, i: (b, 0)),
            pl.BlockSpec((seq // tv, FOX_HEAD_DIM, tv), lambda b, h, i: (b, h, 0)),
        ],
        out_specs=pl.BlockSpec((tq, FOX_HEAD_DIM), lambda b, h, i: (b * nq + i, h)),
        out_shape=jax.ShapeDtypeStruct((t, D_MODEL), BF16),
        scratch_shapes=[
            pltpu.VMEM((4, tk, tq), F32),
            pltpu.VMEM((1, tq), F32),
            pltpu.VMEM((FOX_HEAD_DIM + ONES_ROWS, tq), F32),
        ],
        compiler_params=_params("arbitrary", "arbitrary", "arbitrary"),
        name="fox_attn",
    )(qk, qk, ca, vt)


def _out_proj_kernel(a_ref, w_ref, r_ref, o_ref):
    o_ref[...] = r_ref[...] + _dot(a_ref[...], w_ref[...])


def _out_proj(a, w, resid, name):
    t, k = a.shape
    tm = ROW_TILE
    return pl.pallas_call(
        _out_proj_kernel,
        grid=(t // tm,),
        in_specs=[
            pl.BlockSpec((tm, k), lambda i: (i, 0)),
            pl.BlockSpec((k, D_MODEL), lambda i: (0, 0)),
            pl.BlockSpec((tm, D_MODEL), lambda i: (i, 0)),
        ],
        out_specs=pl.BlockSpec((tm, D_MODEL), lambda i: (i, 0)),
        out_shape=jax.ShapeDtypeStruct((t, D_MODEL), F32),
        compiler_params=_params("arbitrary"),
        name=name,
    )(a, w, resid)


def _ffn_kernel(x_ref, g_ref, wg_ref, wu_ref, wo_ref, fg_ref, o_ref, *, final_norm):
    x = x_ref[...]
    h = _rmsnorm(x, g_ref[...]).astype(BF16)
    out = x
    for c0, width in FFN_CHUNKS:
        gate = _dot(h, wg_ref[:, c0:c0 + width])
        up = _dot(h, wu_ref[:, c0:c0 + width])
        act = (gate * jax.nn.sigmoid(gate) * up).astype(BF16)
        out = out + _dot(act, wo_ref[c0:c0 + width, :])
    if final_norm:
        out = _rmsnorm(out, fg_ref[...])
    o_ref[...] = out


def _ffn(x2d, g, w_gate, w_up, w_out, final_g, final_norm, name):
    t = x2d.shape[0]
    tm = ROW_TILE
    return pl.pallas_call(
        functools.partial(_ffn_kernel, final_norm=final_norm),
        grid=(t // tm,),
        in_specs=[
            pl.BlockSpec((tm, D_MODEL), lambda i: (i, 0)),
            pl.BlockSpec((1, D_MODEL), lambda i: (0, 0)),
            pl.BlockSpec((D_MODEL, D_FF), lambda i: (0, 0)),
            pl.BlockSpec((D_MODEL, D_FF), lambda i: (0, 0)),
            pl.BlockSpec((D_FF, D_MODEL), lambda i: (0, 0)),
            pl.BlockSpec((1, D_MODEL), lambda i: (0, 0)),
        ],
        out_specs=pl.BlockSpec((tm, D_MODEL), lambda i: (i, 0)),
        out_shape=jax.ShapeDtypeStruct((t, D_MODEL), F32),
        compiler_params=_params("arbitrary"),
        name=name,
    )(x2d, g, w_gate, w_up, w_out, final_g)


def _ret_proj_kernel(x_ref, g_ref, w_ref, cos_ref, sin_ref, q_ref, k_ref, v_ref, sg_ref):
    h = _rmsnorm(x_ref[...], g_ref[...]).astype(BF16)
    cos = cos_ref[...]
    sin = sin_ref[...]
    half = RET_QK_DIM // 2
    tn = 512

    def rotated(out_ref, base, scale):
        for n0 in range(0, D_MODEL, tn):
            acc = _dot(h, w_ref[:, base + n0:base + n0 + tn])
            for d0 in range(0, tn, RET_QK_DIM):
                t1 = acc[:, d0:d0 + half]
                t2 = acc[:, d0 + half:d0 + RET_QK_DIM]
                out_ref[:, n0 + d0:n0 + d0 + half] = ((t1 * cos - t2 * sin) * scale).astype(BF16)
                out_ref[:, n0 + d0 + half:n0 + d0 + RET_QK_DIM] = ((t1 * sin + t2 * cos) * scale).astype(BF16)

    rotated(q_ref, 0, 1.0)
    rotated(k_ref, D_MODEL, RET_QK_DIM ** -0.5)
    for n0 in range(0, 2 * D_MODEL, tn):
        v_ref[:, n0:n0 + tn] = _dot(h, w_ref[:, 2 * D_MODEL + n0:2 * D_MODEL + n0 + tn]).astype(BF16)
    for n0 in range(0, 2 * D_MODEL, tn):
        gate = _dot(h, w_ref[:, 4 * D_MODEL + n0:4 * D_MODEL + n0 + tn])
        sg_ref[:, n0:n0 + tn] = (gate * jax.nn.sigmoid(gate)).astype(BF16)


def _ret_proj(x2d, g, w, cos, sin, seq):
    t = x2d.shape[0]
    tm = ROW_TILE
    tiles_per_seq = seq // tm
    half = RET_QK_DIM // 2
    row = lambda i: (i, 0)
    return pl.pallas_call(
        _ret_proj_kernel,
        grid=(t // tm,),
        in_specs=[
            pl.BlockSpec((tm, D_MODEL), row),
            pl.BlockSpec((1, D_MODEL), lambda i: (0, 0)),
            pl.BlockSpec((D_MODEL, 6 * D_MODEL), lambda i: (0, 0)),
            pl.BlockSpec((tm, half), lambda i: (i % tiles_per_seq, 0)),
            pl.BlockSpec((tm, half), lambda i: (i % tiles_per_seq, 0)),
        ],
        out_specs=[
            pl.BlockSpec((tm, D_MODEL), row),
            pl.BlockSpec((tm, D_MODEL), row),
            pl.BlockSpec((tm, 2 * D_MODEL), row),
            pl.BlockSpec((tm, 2 * D_MODEL), row),
        ],
        out_shape=[
            jax.ShapeDtypeStruct((t, D_MODEL), BF16),
            jax.ShapeDtypeStruct((t, D_MODEL), BF16),
            jax.ShapeDtypeStruct((t, 2 * D_MODEL), BF16),
            jax.ShapeDtypeStruct((t, 2 * D_MODEL), BF16),
        ],
        compiler_params=_params("arbitrary"),
        name="ret_proj",
    )(x2d, g, w, cos, sin)


def _retention_kernel(q_ref, k_ref, v_ref, sg_ref, w_ref, qd_ref, kd_ref, o_ref, state_ref, *, block_decay):
    n = pl.program_id(2)
    hd = pl.program_id(1)

    @pl.when(n == 0)
    def _():
        state_ref[...] = jnp.zeros_like(state_ref)

    q = q_ref[...]
    k = k_ref[...]
    v = v_ref[...]
    state = state_ref[...]
    scores = (_dot_nt(q, k) * w_ref[...]).astype(BF16)
    y = _dot(scores, v) + _dot(q, state.astype(BF16)) * qd_ref[...]
    k_dec = (k.astype(F32) * kd_ref[...]).astype(BF16)
    decay = jnp.float32(block_decay[0])
    for idx in range(1, RET_HEADS):
        decay = jnp.where(hd == idx, jnp.float32(block_decay[idx]), decay)
    state_ref[...] = state * decay + _dot_tn(k_dec, v)

    mu = jnp.mean(y, axis=-1, keepdims=True)
    yc = y - mu
    var = jnp.mean(yc * yc, axis=-1, keepdims=True)
    o_ref[...] = (sg_ref[...].astype(F32) * (yc * lax.rsqrt(var + EPS))).astype(o_ref.dtype)


def _retention_tables():
    r = RET_BLOCK
    gamma = 1.0 - 2.0 ** (-5.0 - np.arange(RET_HEADS, dtype=np.float64))
    log_gamma = jnp.log(jnp.asarray(gamma, dtype=F32))
    pos = np.arange(r)
    diff = (pos[:, None] - pos[None, :]).astype(np.float32)
    same = (pos[:, None] // CHUNK) == (pos[None, :] // CHUNK)
    earlier = (pos[None, :] // CHUNK) < (pos[:, None] // CHUNK)
    expo = np.where(same, np.abs(diff), diff)
    w = jnp.where(jnp.asarray(same | earlier)[None],
                  jnp.exp(log_gamma[:, None, None] * jnp.asarray(expo)[None]), 0.0)
    idx = jnp.arange(r, dtype=F32)
    qd = jnp.exp(log_gamma[:, None] * (idx[None, :] + 1.0))[..., None]
    kd = jnp.exp(log_gamma[:, None] * (r - 1.0 - idx[None, :]))[..., None]
    block_decay = tuple(float(g) ** r for g in gamma.astype(np.float32).astype(np.float64))
    return w, qd, kd, block_decay


def _retention(q, k, v, sg, batch, seq):
    t = q.shape[0]
    r = RET_BLOCK
    nb = seq // r
    w, qd, kd, block_decay = _retention_tables()
    blk = lambda b, h, n: (b * nb + n, h)
    per_head = lambda b, h, n: (h, 0, 0)
    return pl.pallas_call(
        functools.partial(_retention_kernel, block_decay=block_decay),
        grid=(batch, RET_HEADS, nb),
        in_specs=[
            pl.BlockSpec((r, RET_QK_DIM), blk),
            pl.BlockSpec((r, RET_QK_DIM), blk),
            pl.BlockSpec((r, RET_V_DIM), blk),
            pl.BlockSpec((r, RET_V_DIM), blk),
            pl.BlockSpec((None, r, r), per_head),
            pl.BlockSpec((None, r, 1), per_head),
            pl.BlockSpec((None, r, 1), per_head),
        ],
        out_specs=pl.BlockSpec((r, RET_V_DIM), blk),
        out_shape=jax.ShapeDtypeStruct((t, 2 * D_MODEL), BF16),
        scratch_shapes=[pltpu.VMEM((RET_QK_DIM, RET_V_DIM), F32)],
        compiler_params=_params("arbitrary", "arbitrary", "arbitrary"),
        name="retention",
    )(q, k, v, sg, w, qd, kd)


def kernel(x, norm_mix, norm_ffn, fox_w_in, fox_b_f, fox_w_out, ret_w_in, ret_w_out,
           ffn_w_in, ffn_w_out, final_norm):
    batch, seq, d = x.shape
    t = batch * seq
    x2d = x.reshape(t, d)
    row = lambda a: a.reshape(1, -1)

    w_in = fox_w_in[0]
    w_qk = w_in[:, :2 * D_MODEL].astype(BF16)
    w_vt = w_in[:, 2 * D_MODEL:3 * D_MODEL].T.astype(BF16)
    pad = LANES - C_PARTS * FOX_HEADS
    wf = jnp.pad(jnp.tile(w_in[:, 3 * D_MODEL:], (1, C_PARTS)), ((0, 0), (0, pad))).astype(BF16)
    bf = jnp.pad(jnp.tile(fox_b_f[0], C_PARTS), (0, pad)).reshape(1, LANES)
    qk, vt, ca = _fox_proj(x2d, row(norm_mix[0]), w_qk, w_vt, wf, bf, seq)
    attn = _fox_attn(qk, ca, vt, batch, seq)
    x2d = _out_proj(attn, fox_w_out[0].astype(BF16), x2d, "fox_out")
    x2d = _ffn(x2d, row(norm_ffn[0]), ffn_w_in[0][:, :D_FF].astype(BF16), ffn_w_in[0][:, D_FF:].astype(BF16),
               ffn_w_out[0].astype(BF16), row(final_norm), False, "ffn0")

    half = RET_QK_DIM // 2
    inv = ROPE_BASE ** (-jnp.arange(half, dtype=F32) / half)
    ang = jnp.arange(seq, dtype=F32)[:, None] * inv[None, :]
    q, k, v, sg = _ret_proj(x2d, row(norm_mix[1]), ret_w_in[0].astype(BF16), jnp.cos(ang), jnp.sin(ang), seq)
    gated = _retention(q, k, v, sg, batch, seq)
    x2d = _out_proj(gated, ret_w_out[0].astype(BF16), x2d, "ret_out")
    x2d = _ffn(x2d, row(norm_ffn[1]), ffn_w_in[1][:, :D_FF].astype(BF16), ffn_w_in[1][:, D_FF:].astype(BF16),
               ffn_w_out[1].astype(BF16), row(final_norm), True, "ffn1")
    return x2d.reshape(batch, seq, d)
```

```python
import functools
import math

import jax
import jax.numpy as jnp
import numpy as np
from jax import lax
from jax.experimental import pallas as pl
from jax.experimental.pallas import tpu as pltpu

D_MODEL = 1024
CHUNK = 64
FOX_HEADS = 8
FOX_HEAD_DIM = D_MODEL // FOX_HEADS
RET_HEADS = 4
RET_QK_DIM = D_MODEL // RET_HEADS
RET_V_DIM = 2 * D_MODEL // RET_HEADS
D_FF = -(-8 * D_MODEL // (3 * 256)) * 256
ROPE_BASE = 10000.0
EPS = 1e-6
LOG2E = math.log2(math.e)

F32 = jnp.float32
BF16 = jnp.bfloat16

VMEM_LIMIT_BYTES = 56 * 1024 * 1024
LANES = 128

ROW_TILE = 512
ATTN_Q_TILE = 2 * ROW_TILE
ATTN_K_TILE = ATTN_Q_TILE // 4
RET_BLOCK = 256
FFN_CHUNKS = ((0, 1024), (1024, 1024), (2048, 768))
C_PARTS = 3
ONES_ROWS = 16


def _params(*semantics):
    return pltpu.CompilerParams(dimension_semantics=semantics, vmem_limit_bytes=VMEM_LIMIT_BYTES)


def _rmsnorm(x, g):
    return x * lax.rsqrt(jnp.mean(x * x, axis=-1, keepdims=True) + EPS) * g


def _dot(a, b):
    return jnp.dot(a, b, preferred_element_type=F32)


def _dot_nt(a, b):
    return lax.dot_general(a, b, (((1,), (1,)), ((), ())), preferred_element_type=F32)


def _dot_tn(a, b):
    return lax.dot_general(a, b, (((0,), (0,)), ((), ())), preferred_element_type=F32)


def _cumsum_rows(v):
    n = v.shape[0]
    row = lax.broadcasted_iota(jnp.int32, v.shape, 0)
    shift = 1
    while shift < n:
        rolled = pltpu.roll(v, shift, axis=0)
        v = v + jnp.where(row >= shift, rolled, 0.0)
        shift *= 2
    return v


def _fox_proj_kernel(x_ref, g_ref, wqk_ref, wvt_ref, wf_ref, bf_ref, qk_ref, vt_ref, ca_ref, carry_ref,
                     *, tiles_per_seq):
    i = pl.program_id(0)
    tm = x_ref.shape[0]
    h = _rmsnorm(x_ref[...], g_ref[...]).astype(BF16)
    q_scale = FOX_HEAD_DIM ** -0.5 * LOG2E
    tn = 512
    for n0 in range(0, 2 * D_MODEL, tn):
        acc = _dot(h, wqk_ref[:, n0:n0 + tn])
        if n0 < D_MODEL:
            acc = acc * q_scale
        qk_ref[:, n0:n0 + tn] = acc.astype(BF16)
    for r0 in range(0, D_MODEL, tn):
        vt_ref[r0:r0 + tn, :] = _dot_nt(wvt_ref[r0:r0 + tn, :], h).astype(BF16)

    logits = _dot(h, wf_ref[...]) + bf_ref[...]
    log_f = jnp.minimum(logits, 0.0) - jnp.log1p(jnp.exp(-jnp.abs(logits)))

    @pl.when(i % tiles_per_seq == 0)
    def _():
        carry_ref[...] = jnp.zeros_like(carry_ref)

    c = _cumsum_rows(log_f) + carry_ref[...]
    carry_ref[...] = c[tm - 1:tm, :]
    neg_c = c * (-LOG2E)
    hi = neg_c.astype(BF16)
    rest = neg_c - hi.astype(F32)
    mid = rest.astype(BF16)
    lo = (rest - mid.astype(F32)).astype(BF16)
    lane = lax.broadcasted_iota(jnp.int32, neg_c.shape, 1)
    ca_ref[...] = jnp.where(lane < FOX_HEADS, hi, jnp.where(lane < 2 * FOX_HEADS, mid, lo))


def _fox_proj(x2d, g, w_qk, w_vt, wf, bf, seq):
    t = x2d.shape[0]
    tm = ROW_TILE
    return pl.pallas_call(
        functools.partial(_fox_proj_kernel, tiles_per_seq=seq // tm),
        grid=(t // tm,),
        in_specs=[
            pl.BlockSpec((tm, D_MODEL), lambda i: (i, 0)),
            pl.BlockSpec((1, D_MODEL), lambda i: (0, 0)),
            pl.BlockSpec((D_MODEL, 2 * D_MODEL), lambda i: (0, 0)),
            pl.BlockSpec((D_MODEL, D_MODEL), lambda i: (0, 0)),
            pl.BlockSpec((D_MODEL, LANES), lambda i: (0, 0)),
            pl.BlockSpec((1, LANES), lambda i: (0, 0)),
        ],
        out_specs=[
            pl.BlockSpec((tm, 2 * D_MODEL), lambda i: (i, 0)),
            pl.BlockSpec((None, D_MODEL, tm), lambda i: (i, 0, 0)),
            pl.BlockSpec((tm, LANES), lambda i: (i, 0)),
        ],
        out_shape=[
            jax.ShapeDtypeStruct((t, 2 * D_MODEL), BF16),
            jax.ShapeDtypeStruct((t // tm, D_MODEL, tm), BF16),
            jax.ShapeDtypeStruct((t, LANES), BF16),
        ],
        scratch_shapes=[pltpu.VMEM((1, LANES), F32)],
        compiler_params=_params("arbitrary"),
        name="fox_proj",
    )(x2d, g, w_qk, w_vt, wf, bf)


def _fox_attn_kernel(q_ref, k_ref, ca_ref, vt_ref, o_ref, s_ref, m_ref, acc_ref):
    head = pl.program_id(1)
    i = pl.program_id(2)
    tq = q_ref.shape[0]
    tk = ATTN_K_TILE
    tv = vt_ref.shape[2]
    assert tq == 4 * tk and tq == 2 * tv

    lane = lax.broadcasted_iota(jnp.int32, (tq, LANES), 1)
    pick = jnp.logical_and(lane % FOX_HEADS == head, lane < C_PARTS * FOX_HEADS)
    q_ext = jnp.concatenate([q_ref[...], jnp.where(pick, 1.0, 0.0).astype(BF16)], axis=1)

    def scores(row0, c0=0):
        k_ext = jnp.concatenate([k_ref[pl.ds(row0, tk), :], ca_ref[pl.ds(row0, tk), :]], axis=1)
        return _dot_nt(k_ext, q_ext[c0:, :])

    def update(tiles, vt, c0=0, diagonal=False):
        if diagonal:
            (s,) = tiles
            key = lax.broadcasted_iota(jnp.int32, s.shape, 0)
            qry = lax.broadcasted_iota(jnp.int32, s.shape, 1)
            tiles = [jnp.where(qry >= key, s, -jnp.inf)]
        m_old = m_ref[:, c0:]
        m_new = m_old
        for s in tiles:
            m_new = jnp.maximum(m_new, jnp.max(s, axis=0, keepdims=True))
        alpha = jnp.exp2(m_old - m_new)
        p = [jnp.exp2(s - m_new).astype(BF16) for s in tiles]
        p = p[0] if len(p) == 1 else jnp.concatenate(p, axis=0)
        vt_ext = jnp.concatenate([vt, jnp.ones((ONES_ROWS, vt.shape[1]), BF16)], axis=0)
        acc_ref[:, c0:] = alpha * acc_ref[:, c0:] + _dot(vt_ext, p)
        m_ref[:, c0:] = m_new

    m_ref[...] = jnp.full_like(m_ref, -jnp.inf)
    acc_ref[...] = jnp.zeros_like(acc_ref)
    s_ref[0] = scores(0)
    s_ref[1] = scores(tk)

    def body(j, carry):
        base = pl.multiple_of(j * tq, tq)
        s_ref[2] = scores(base + 2 * tk)
        s_ref[3] = scores(base + 3 * tk)
        update([s_ref[0], s_ref[1]], vt_ref[2 * j])
        s_ref[0] = scores(base + tq)
        s_ref[1] = scores(base + tq + tk)
        update([s_ref[2], s_ref[3]], vt_ref[2 * j + 1])
        return carry

    lax.fori_loop(0, i, body, 0)
    base = pl.multiple_of(i * tq, tq)
    s2 = scores(base + 2 * tk, 2 * tk)
    s3 = scores(base + 3 * tk, 3 * tk)
    update([s_ref[0]], vt_ref[2 * i, :, 0:tk], 0, True)
    update([s_ref[1, :, tk:]], vt_ref[2 * i, :, tk:tv], tk, True)
    update([s2], vt_ref[2 * i + 1, :, 0:tk], 2 * tk, True)
    update([s3], vt_ref[2 * i + 1, :, tk:tv], 3 * tk, True)
    acc = acc_ref[...]
    out = acc[:FOX_HEAD_DIM] / acc[FOX_HEAD_DIM:FOX_HEAD_DIM + 1]
    o_ref[...] = jnp.transpose(out).astype(o_ref.dtype)


def _fox_attn(qk, ca, vt, batch, seq):
    t = qk.shape[0]
    tq = ATTN_Q_TILE
    tk = ATTN_K_TILE
    nq = seq // tq
    tv = vt.shape[2]
    hb = D_MODEL // FOX_HEAD_DIM
    return pl.pallas_call(
        _fox_attn_kernel,
        grid=(batch, FOX_HEADS, nq),
        in_specs=[
            pl.BlockSpec((tq, FOX_HEAD_DIM), lambda b, h, i: (b * nq + i, h)),
            pl.BlockSpec((seq, FOX_HEAD_DIM), lambda b, h, i: (b, hb + h)),
            pl.BlockSpec((seq, LANES), lambda b, h, i: (b, 0)),
            pl.BlockSpec((seq // tv, FOX_HEAD_DIM, tv), lambda b, h, i: (b, h, 0)),
        ],
        out_specs=pl.BlockSpec((tq, FOX_HEAD_DIM), lambda b, h, i: (b * nq + i, h)),
        out_shape=jax.ShapeDtypeStruct((t, D_MODEL), BF16),
        scratch_shapes=[
            pltpu.VMEM((4, tk, tq), F32),
            pltpu.VMEM((1, tq), F32),
            pltpu.VMEM((FOX_HEAD_DIM + ONES_ROWS, tq), F32),
        ],
        compiler_params=_params("arbitrary", "arbitrary", "arbitrary"),
        name="fox_attn",
    )(qk, qk, ca, vt)


def _ffn_kernel(a_ref, wp_ref, x_ref, g_ref, wg_ref, wu_ref, wo_ref, fg_ref, o_ref, *, final_norm):
    x = x_ref[...] + _dot(a_ref[...], wp_ref[...])
    h = _rmsnorm(x, g_ref[...]).astype(BF16)
    out = x
    for c0, width in FFN_CHUNKS:
        gate = _dot(h, wg_ref[:, c0:c0 + width])
        up = _dot(h, wu_ref[:, c0:c0 + width])
        act = (gate * jax.nn.sigmoid(gate) * up).astype(BF16)
        out = out + _dot(act, wo_ref[c0:c0 + width, :])
    if final_norm:
        out = _rmsnorm(out, fg_ref[...])
    o_ref[...] = out


def _ffn(mixed, w_proj, x2d, g, w_gate, w_up, w_out, final_g, final_norm, name):
    t = x2d.shape[0]
    k = mixed.shape[1]
    tm = ROW_TILE
    return pl.pallas_call(
        functools.partial(_ffn_kernel, final_norm=final_norm),
        grid=(t // tm,),
        in_specs=[
            pl.BlockSpec((tm, k), lambda i: (i, 0)),
            pl.BlockSpec((k, D_MODEL), lambda i: (0, 0)),
            pl.BlockSpec((tm, D_MODEL), lambda i: (i, 0)),
            pl.BlockSpec((1, D_MODEL), lambda i: (0, 0)),
            pl.BlockSpec((D_MODEL, D_FF), lambda i: (0, 0)),
            pl.BlockSpec((D_MODEL, D_FF), lambda i: (0, 0)),
            pl.BlockSpec((D_FF, D_MODEL), lambda i: (0, 0)),
            pl.BlockSpec((1, D_MODEL), lambda i: (0, 0)),
        ],
        out_specs=pl.BlockSpec((tm, D_MODEL), lambda i: (i, 0)),
        out_shape=jax.ShapeDtypeStruct((t, D_MODEL), F32),
        compiler_params=_params("arbitrary"),
        name=name,
    )(mixed, w_proj, x2d, g, w_gate, w_up, w_out, final_g)


def _ret_proj_kernel(x_ref, g_ref, wq_ref, wkt_ref, wvg_ref, cos_ref, sin_ref, cost_ref, sint_ref,
                     q_ref, kt_ref, v_ref, sg_ref):
    h = _rmsnorm(x_ref[...], g_ref[...]).astype(BF16)
    half = RET_QK_DIM // 2
    tn = 512
    cos, sin = cos_ref[...], sin_ref[...]
    for n0 in range(0, D_MODEL, tn):
        acc = _dot(h, wq_ref[:, n0:n0 + tn])
        for d0 in range(0, tn, RET_QK_DIM):
            t1 = acc[:, d0:d0 + half]
            t2 = acc[:, d0 + half:d0 + RET_QK_DIM]
            q_ref[:, n0 + d0:n0 + d0 + half] = (t1 * cos - t2 * sin).astype(BF16)
            q_ref[:, n0 + d0 + half:n0 + d0 + RET_QK_DIM] = (t1 * sin + t2 * cos).astype(BF16)
    cos_t, sin_t = cost_ref[...], sint_ref[...]
    k_scale = RET_QK_DIM ** -0.5
    for r0 in range(0, D_MODEL, RET_QK_DIM):
        acc = _dot_nt(wkt_ref[r0:r0 + RET_QK_DIM, :], h)
        t1 = acc[:half]
        t2 = acc[half:]
        kt_ref[r0:r0 + half, :] = ((t1 * cos_t - t2 * sin_t) * k_scale).astype(BF16)
        kt_ref[r0 + half:r0 + RET_QK_DIM, :] = ((t1 * sin_t + t2 * cos_t) * k_scale).astype(BF16)
    for n0 in range(0, 2 * D_MODEL, tn):
        v_ref[:, n0:n0 + tn] = _dot(h, wvg_ref[:, n0:n0 + tn]).astype(BF16)
    for n0 in range(0, 2 * D_MODEL, tn):
        gate = _dot(h, wvg_ref[:, 2 * D_MODEL + n0:2 * D_MODEL + n0 + tn])
        sg_ref[:, n0:n0 + tn] = (gate * jax.nn.sigmoid(gate)).astype(BF16)


def _ret_proj(x2d, g, w_q, w_kt, w_vg, cos, sin, seq):
    t = x2d.shape[0]
    tm = ROW_TILE
    tiles_per_seq = seq // tm
    half = RET_QK_DIM // 2
    row = lambda i: (i, 0)
    const = lambda i: (0, 0)
    return pl.pallas_call(
        _ret_proj_kernel,
        grid=(t // tm,),
        in_specs=[
            pl.BlockSpec((tm, D_MODEL), row),
            pl.BlockSpec((1, D_MODEL), const),
            pl.BlockSpec((D_MODEL, D_MODEL), const),
            pl.BlockSpec((D_MODEL, D_MODEL), const),
            pl.BlockSpec((D_MODEL, 4 * D_MODEL), const),
            pl.BlockSpec((tm, half), lambda i: (i % tiles_per_seq, 0)),
            pl.BlockSpec((tm, half), lambda i: (i % tiles_per_seq, 0)),
            pl.BlockSpec((half, tm), lambda i: (0, i % tiles_per_seq)),
            pl.BlockSpec((half, tm), lambda i: (0, i % tiles_per_seq)),
        ],
        out_specs=[
            pl.BlockSpec((tm, D_MODEL), row),
            pl.BlockSpec((None, D_MODEL, tm), lambda i: (i, 0, 0)),
            pl.BlockSpec((tm, 2 * D_MODEL), row),
            pl.BlockSpec((tm, 2 * D_MODEL), row),
        ],
        out_shape=[
            jax.ShapeDtypeStruct((t, D_MODEL), BF16),
            jax.ShapeDtypeStruct((t // tm, D_MODEL, tm), BF16),
            jax.ShapeDtypeStruct((t, 2 * D_MODEL), BF16),
            jax.ShapeDtypeStruct((t, 2 * D_MODEL), BF16),
        ],
        compiler_params=_params("arbitrary"),
        name="ret_proj",
    )(x2d, g, w_q, w_kt, w_vg, cos, sin, cos.T, sin.T)


def _retention_kernel(q_ref, kt_ref, v_ref, sg_ref, w_ref, qd_ref, kd_ref, o_ref, state_ref, *, block_decay):
    @pl.when(pl.program_id(1) == 0)
    def _():
        state_ref[...] = jnp.zeros_like(state_ref)

    for hd in range(RET_HEADS):
        qk_cols = slice(hd * RET_QK_DIM, (hd + 1) * RET_QK_DIM)
        v_cols = slice(hd * RET_V_DIM, (hd + 1) * RET_V_DIM)
        q = q_ref[:, qk_cols]
        kt = kt_ref[qk_cols, :]
        v = v_ref[:, v_cols]
        state = state_ref[hd]
        p = (_dot(q, kt) * w_ref[hd]).astype(BF16)
        q_dec = (q.astype(F32) * qd_ref[hd]).astype(BF16)
        y = _dot(jnp.concatenate([p, q_dec], axis=1), jnp.concatenate([v, state.astype(BF16)], axis=0))
        kt_dec = (kt.astype(F32) * kd_ref[hd]).astype(BF16)
        state_ref[hd] = state * block_decay[hd] + _dot(kt_dec, v)

        mu = jnp.mean(y, axis=-1, keepdims=True)
        yc = y - mu
        var = jnp.mean(yc * yc, axis=-1, keepdims=True)
        gated = sg_ref[:, v_cols].astype(F32) * (yc * lax.rsqrt(var + EPS))
        o_ref[:, v_cols] = gated.astype(o_ref.dtype)


def _retention_tables():
    r = RET_BLOCK
    gamma = 1.0 - 2.0 ** (-5.0 - np.arange(RET_HEADS, dtype=np.float64))
    log_gamma = jnp.log(jnp.asarray(gamma, dtype=F32))
    pos = np.arange(r)
    diff = (pos[:, None] - pos[None, :]).astype(np.float32)
    same = (pos[:, None] // CHUNK) == (pos[None, :] // CHUNK)
    earlier = (pos[None, :] // CHUNK) < (pos[:, None] // CHUNK)
    expo = np.where(same, np.abs(diff), diff)
    w = jnp.where(jnp.asarray(same | earlier)[None],
                  jnp.exp(log_gamma[:, None, None] * jnp.asarray(expo)[None]), 0.0)
    idx = jnp.arange(r, dtype=F32)
    qd = jnp.exp(log_gamma[:, None] * (idx[None, :] + 1.0))
    qd = jnp.broadcast_to(qd[..., None], (RET_HEADS, r, RET_QK_DIM))
    kd = jnp.exp(log_gamma[:, None] * (r - 1.0 - idx[None, :]))[:, None, :]
    block_decay = tuple(float(g) ** r for g in gamma.astype(np.float32).astype(np.float64))
    return w, qd, kd, block_decay


def _retention(q, kt, v, sg, batch, seq):
    t = q.shape[0]
    r = RET_BLOCK
    nb = seq // r
    per_slab = kt.shape[2] // r
    w, qd, kd, block_decay = _retention_tables()
    blk = lambda b, n: (b * nb + n, 0)
    const = lambda b, n: (0, 0, 0)
    return pl.pallas_call(
        functools.partial(_retention_kernel, block_decay=block_decay),
        grid=(batch, nb),
        in_specs=[
            pl.BlockSpec((r, D_MODEL), blk),
            pl.BlockSpec((None, D_MODEL, r), lambda b, n: ((b * nb + n) // per_slab, 0, (b * nb + n) % per_slab)),
            pl.BlockSpec((r, 2 * D_MODEL), blk),
            pl.BlockSpec((r, 2 * D_MODEL), blk),
            pl.BlockSpec((RET_HEADS, r, r), const),
            pl.BlockSpec((RET_HEADS, r, RET_QK_DIM), const),
            pl.BlockSpec((RET_HEADS, 1, r), const),
        ],
        out_specs=pl.BlockSpec((r, 2 * D_MODEL), blk),
        out_shape=jax.ShapeDtypeStruct((t, 2 * D_MODEL), BF16),
        scratch_shapes=[pltpu.VMEM((RET_HEADS, RET_QK_DIM, RET_V_DIM), F32)],
        compiler_params=_params("arbitrary", "arbitrary"),
        name="retention",
    )(q, kt, v, sg, w, qd, kd)


def kernel(x, norm_mix, norm_ffn, fox_w_in, fox_b_f, fox_w_out, ret_w_in, ret_w_out,
           ffn_w_in, ffn_w_out, final_norm):
    batch, seq, d = x.shape
    t = batch * seq
    x2d = x.reshape(t, d)
    row = lambda a: a.reshape(1, -1)

    w_in = fox_w_in[0]
    w_qk = w_in[:, :2 * D_MODEL].astype(BF16)
    w_vt = w_in[:, 2 * D_MODEL:3 * D_MODEL].T.astype(BF16)
    pad = LANES - C_PARTS * FOX_HEADS
    wf = jnp.pad(jnp.tile(w_in[:, 3 * D_MODEL:], (1, C_PARTS)), ((0, 0), (0, pad))).astype(BF16)
    bf = jnp.pad(jnp.tile(fox_b_f[0], C_PARTS), (0, pad)).reshape(1, LANES)
    qk, vt, ca = _fox_proj(x2d, row(norm_mix[0]), w_qk, w_vt, wf, bf, seq)
    attn = _fox_attn(qk, ca, vt, batch, seq)
    x2d = _ffn(attn, fox_w_out[0].astype(BF16), x2d, row(norm_ffn[0]),
               ffn_w_in[0][:, :D_FF].astype(BF16), ffn_w_in[0][:, D_FF:].astype(BF16),
               ffn_w_out[0].astype(BF16), row(final_norm), False, "ffn0")

    half = RET_QK_DIM // 2
    inv = ROPE_BASE ** (-jnp.arange(half, dtype=F32) / half)
    ang = jnp.arange(seq, dtype=F32)[:, None] * inv[None, :]
    w_in = ret_w_in[0]
    q, kt, v, sg = _ret_proj(x2d, row(norm_mix[1]), w_in[:, :D_MODEL].astype(BF16),
                             w_in[:, D_MODEL:2 * D_MODEL].T.astype(BF16), w_in[:, 2 * D_MODEL:].astype(BF16),
                             jnp.cos(ang), jnp.sin(ang), seq)
    gated = _retention(q, kt, v, sg, batch, seq)
    x2d = _ffn(gated, ret_w_out[0].astype(BF16), x2d, row(norm_ffn[1]),
               ffn_w_in[1][:, :D_FF].astype(BF16), ffn_w_in[1][:, D_FF:].astype(BF16),
               ffn_w_out[1].astype(BF16), row(final_norm), True, "ffn1")
    return x2d.reshape(batch, seq, d)
```

```python
import functools
import math

import jax
import jax.numpy as jnp
import numpy as np
from jax import lax
from jax.experimental import pallas as pl
from jax.experimental.pallas import tpu as pltpu

D_MODEL = 1024
CHUNK = 64
FOX_HEADS = 8
FOX_HEAD_DIM = D_MODEL // FOX_HEADS
RET_HEADS = 4
RET_QK_DIM = D_MODEL // RET_HEADS
RET_V_DIM = 2 * D_MODEL // RET_HEADS
D_FF = -(-8 * D_MODEL // (3 * 256)) * 256
ROPE_BASE = 10000.0
EPS = 1e-6
LOG2E = math.log2(math.e)

F32 = jnp.float32
BF16 = jnp.bfloat16

VMEM_LIMIT_BYTES = 56 * 1024 * 1024
LANES = 128

ROW_TILE = 512
ATTN_Q_TILE = 2 * ROW_TILE
ATTN_K_TILE = ATTN_Q_TILE // 4
RET_BLOCK = 256
FFN_CHUNKS = ((0, 1024), (1024, 1024), (2048, 768))
C_PARTS = 3
ONES_ROWS = 16


def _params(*semantics):
    return pltpu.CompilerParams(dimension_semantics=semantics, vmem_limit_bytes=VMEM_LIMIT_BYTES)


def _rmsnorm(x, g):
    return x * lax.rsqrt(jnp.mean(x * x, axis=-1, keepdims=True) + EPS) * g


def _dot(a, b):
    return jnp.dot(a, b, preferred_element_type=F32)


def _dot_nt(a, b):
    return lax.dot_general(a, b, (((1,), (1,)), ((), ())), preferred_element_type=F32)


def _dot_tn(a, b):
    return lax.dot_general(a, b, (((0,), (0,)), ((), ())), preferred_element_type=F32)


def _cumsum_rows(v):
    n = v.shape[0]
    row = lax.broadcasted_iota(jnp.int32, v.shape, 0)
    shift = 1
    while shift < n:
        rolled = pltpu.roll(v, shift, axis=0)
        v = v + jnp.where(row >= shift, rolled, 0.0)
        shift *= 2
    return v


def _fox_proj_kernel(x_ref, g_ref, wk_ref, wqvt_ref, wf_ref, bf_ref, k_ref, qvt_ref, ca_ref, carry_ref,
                     *, tiles_per_seq):
    i = pl.program_id(0)
    tm = x_ref.shape[0]
    h = _rmsnorm(x_ref[...], g_ref[...]).astype(BF16)
    q_scale = FOX_HEAD_DIM ** -0.5 * LOG2E
    tn = 512
    for n0 in range(0, D_MODEL, tn):
        k_ref[:, n0:n0 + tn] = _dot(h, wk_ref[:, n0:n0 + tn]).astype(BF16)
    for r0 in range(0, 2 * D_MODEL, tn):
        acc = _dot_nt(wqvt_ref[r0:r0 + tn, :], h)
        if r0 < D_MODEL:
            acc = acc * q_scale
        qvt_ref[r0:r0 + tn, :] = acc.astype(BF16)

    logits = _dot(h, wf_ref[...]) + bf_ref[...]
    log_f = jnp.minimum(logits, 0.0) - jnp.log1p(jnp.exp(-jnp.abs(logits)))

    @pl.when(i % tiles_per_seq == 0)
    def _():
        carry_ref[...] = jnp.zeros_like(carry_ref)

    c = _cumsum_rows(log_f) + carry_ref[...]
    carry_ref[...] = c[tm - 1:tm, :]
    neg_c = c * (-LOG2E)
    hi = neg_c.astype(BF16)
    rest = neg_c - hi.astype(F32)
    mid = rest.astype(BF16)
    lo = (rest - mid.astype(F32)).astype(BF16)
    lane = lax.broadcasted_iota(jnp.int32, neg_c.shape, 1)
    ca_ref[...] = jnp.where(lane < FOX_HEADS, hi, jnp.where(lane < 2 * FOX_HEADS, mid, lo))


def _fox_proj(x2d, g, w_k, w_qvt, wf, bf, seq):
    t = x2d.shape[0]
    tm = ROW_TILE
    return pl.pallas_call(
        functools.partial(_fox_proj_kernel, tiles_per_seq=seq // tm),
        grid=(t // tm,),
        in_specs=[
            pl.BlockSpec((tm, D_MODEL), lambda i: (i, 0)),
            pl.BlockSpec((1, D_MODEL), lambda i: (0, 0)),
            pl.BlockSpec((D_MODEL, D_MODEL), lambda i: (0, 0)),
            pl.BlockSpec((2 * D_MODEL, D_MODEL), lambda i: (0, 0)),
            pl.BlockSpec((D_MODEL, LANES), lambda i: (0, 0)),
            pl.BlockSpec((1, LANES), lambda i: (0, 0)),
        ],
        out_specs=[
            pl.BlockSpec((tm, D_MODEL), lambda i: (i, 0)),
            pl.BlockSpec((None, 2 * D_MODEL, tm), lambda i: (i, 0, 0)),
            pl.BlockSpec((tm, LANES), lambda i: (i, 0)),
        ],
        out_shape=[
            jax.ShapeDtypeStruct((t, D_MODEL), BF16),
            jax.ShapeDtypeStruct((t // tm, 2 * D_MODEL, tm), BF16),
            jax.ShapeDtypeStruct((t, LANES), BF16),
        ],
        scratch_shapes=[pltpu.VMEM((1, LANES), F32)],
        compiler_params=_params("arbitrary"),
        name="fox_proj",
    )(x2d, g, w_k, w_qvt, wf, bf)


def _fox_attn_kernel(qt_ref, k_ref, ca_ref, vt_ref, o_ref, s_ref, smax_ref, m_ref, acc_ref):
    head = pl.program_id(1)
    tq = ATTN_Q_TILE
    tk = ATTN_K_TILE
    tv = vt_ref.shape[2]
    nq = (qt_ref.shape[0] * tv) // tq
    assert tq == 4 * tk and tq == 2 * tv

    piece = lax.broadcasted_iota(jnp.int32, (LANES, tq), 0)
    pick = jnp.logical_and(piece % FOX_HEADS == head, piece < C_PARTS * FOX_HEADS)
    ones_rows = jnp.where(pick, 1.0, 0.0).astype(BF16)

    def queries(i):
        return jnp.concatenate([jnp.concatenate([qt_ref[2 * i], qt_ref[2 * i + 1]], axis=1), ones_rows], axis=0)

    def scores(qt_ext, row0, c0=0):
        k_ext = jnp.concatenate([k_ref[pl.ds(row0, tk), :], ca_ref[pl.ds(row0, tk), :]], axis=1)
        return _dot(k_ext, qt_ext[:, c0:])

    def update(tiles, vt, c0=0, key_offsets=None, tile_maxes=None):
        if key_offsets is not None:
            key = lax.broadcasted_iota(jnp.int32, tiles[0].shape, 0)
            qry = lax.broadcasted_iota(jnp.int32, tiles[0].shape, 1)
            tiles = [jnp.where(qry >= key + off, s, -jnp.inf) for s, off in zip(tiles, key_offsets)]
        m_old = m_ref[:, c0:]
        m_new = m_old
        for idx, s in enumerate(tiles):
            tile_max = jnp.max(s, axis=0, keepdims=True) if tile_maxes is None else tile_maxes[idx]
            m_new = jnp.maximum(m_new, tile_max)
        alpha = jnp.exp2(m_old - m_new)
        p = [jnp.exp2(s - m_new).astype(BF16) for s in tiles]
        p = p[0] if len(p) == 1 else jnp.concatenate(p, axis=0)
        vt_ext = jnp.concatenate([vt, jnp.ones((ONES_ROWS, vt.shape[1]), BF16)], axis=0)
        acc_ref[:, c0:] = alpha * acc_ref[:, c0:] + _dot(vt_ext, p)
        m_ref[:, c0:] = m_new

    def stash(slot, qt_ext, row0):
        s = scores(qt_ext, row0)
        s_ref[slot] = s
        smax_ref[slot] = jnp.max(s, axis=0, keepdims=True)

    def query_tile(i, carry):
        qt_ext = queries(i)
        m_ref[...] = jnp.full_like(m_ref, -jnp.inf)
        acc_ref[...] = jnp.zeros_like(acc_ref)

        def body(j, carry):
            base = pl.multiple_of(j * tq, tq)
            stash(2, qt_ext, base + 2 * tk)
            stash(3, qt_ext, base + 3 * tk)
            update([s_ref[0], s_ref[1]], vt_ref[2 * j], tile_maxes=[smax_ref[0], smax_ref[1]])
            stash(0, qt_ext, base + tq)
            stash(1, qt_ext, base + tq + tk)
            update([s_ref[2], s_ref[3]], vt_ref[2 * j + 1], tile_maxes=[smax_ref[2], smax_ref[3]])
            return carry

        lax.fori_loop(0, i, body, 0)
        base = pl.multiple_of(i * tq, tq)
        s2 = scores(qt_ext, base + 2 * tk, 2 * tk)
        s3 = scores(qt_ext, base + 3 * tk, 2 * tk)
        update([s_ref[0], s_ref[1]], vt_ref[2 * i], 0, key_offsets=[0, tk])
        qt_next = queries(jnp.minimum(i + 1, nq - 1))
        stash(0, qt_next, 0)
        stash(1, qt_next, tk)
        update([s2, s3], vt_ref[2 * i + 1], 2 * tk, key_offsets=[0, tk])
        acc = acc_ref[...]
        out = acc[:FOX_HEAD_DIM] / acc[FOX_HEAD_DIM:FOX_HEAD_DIM + 1]
        o_ref[pl.ds(base, tq), :] = jnp.transpose(out).astype(o_ref.dtype)
        return carry

    first = queries(0)
    stash(0, first, 0)
    stash(1, first, tk)
    lax.fori_loop(0, nq, query_tile, 0)


def _fox_attn(k, ca, qvt, batch, seq):
    t = k.shape[0]
    tq = ATTN_Q_TILE
    tk = ATTN_K_TILE
    tv = qvt.shape[2]
    hb = D_MODEL // FOX_HEAD_DIM
    return pl.pallas_call(
        _fox_attn_kernel,
        grid=(batch, FOX_HEADS),
        in_specs=[
            pl.BlockSpec((seq // tv, FOX_HEAD_DIM, tv), lambda b, h: (b, h, 0)),
            pl.BlockSpec((seq, FOX_HEAD_DIM), lambda b, h: (b, h)),
            pl.BlockSpec((seq, LANES), lambda b, h: (b, 0)),
            pl.BlockSpec((seq // tv, FOX_HEAD_DIM, tv), lambda b, h: (b, hb + h, 0)),
        ],
        out_specs=pl.BlockSpec((seq, FOX_HEAD_DIM), lambda b, h: (b, h)),
        out_shape=jax.ShapeDtypeStruct((t, D_MODEL), BF16),
        scratch_shapes=[
            pltpu.VMEM((4, tk, tq), F32),
            pltpu.VMEM((4, 1, tq), F32),
            pltpu.VMEM((1, tq), F32),
            pltpu.VMEM((FOX_HEAD_DIM + ONES_ROWS, tq), F32),
        ],
        compiler_params=_params("arbitrary", "arbitrary"),
        name="fox_attn",
    )(qvt, k, ca, qvt)


def _ffn_kernel(a_ref, wp_ref, x_ref, g_ref, wg_ref, wu_ref, wo_ref, fg_ref, o_ref, *, final_norm):
    x = x_ref[...] + _dot(a_ref[...], wp_ref[...])
    h = _rmsnorm(x, g_ref[...]).astype(BF16)
    out = x
    for c0, width in FFN_CHUNKS:
        gate = _dot(h, wg_ref[:, c0:c0 + width])
        up = _dot(h, wu_ref[:, c0:c0 + width])
        act = (gate * jax.nn.sigmoid(gate) * up).astype(BF16)
        out = out + _dot(act, wo_ref[c0:c0 + width, :])
    if final_norm:
        out = _rmsnorm(out, fg_ref[...])
    o_ref[...] = out


def _ffn(mixed, w_proj, x2d, g, w_gate, w_up, w_out, final_g, final_norm, name):
    t = x2d.shape[0]
    k = mixed.shape[1]
    tm = ROW_TILE
    return pl.pallas_call(
        functools.partial(_ffn_kernel, final_norm=final_norm),
        grid=(t // tm,),
        in_specs=[
            pl.BlockSpec((tm, k), lambda i: (i, 0)),
            pl.BlockSpec((k, D_MODEL), lambda i: (0, 0)),
            pl.BlockSpec((tm, D_MODEL), lambda i: (i, 0)),
            pl.BlockSpec((1, D_MODEL), lambda i: (0, 0)),
            pl.BlockSpec((D_MODEL, D_FF), lambda i: (0, 0)),
            pl.BlockSpec((D_MODEL, D_FF), lambda i: (0, 0)),
            pl.BlockSpec((D_FF, D_MODEL), lambda i: (0, 0)),
            pl.BlockSpec((1, D_MODEL), lambda i: (0, 0)),
        ],
        out_specs=pl.BlockSpec((tm, D_MODEL), lambda i: (i, 0)),
        out_shape=jax.ShapeDtypeStruct((t, D_MODEL), F32),
        compiler_params=_params("arbitrary"),
        name=name,
    )(mixed, w_proj, x2d, g, w_gate, w_up, w_out, final_g)


def _ret_proj_kernel(x_ref, g_ref, wq_ref, wkt_ref, wvg_ref, cos_ref, sin_ref, cost_ref, sint_ref,
                     q_ref, kt_ref, v_ref, sg_ref):
    h = _rmsnorm(x_ref[...], g_ref[...]).astype(BF16)
    half = RET_QK_DIM // 2
    tn = 512
    cos, sin = cos_ref[...], sin_ref[...]
    for n0 in range(0, D_MODEL, tn):
        acc = _dot(h, wq_ref[:, n0:n0 + tn])
        for d0 in range(0, tn, RET_QK_DIM):
            t1 = acc[:, d0:d0 + half]
            t2 = acc[:, d0 + half:d0 + RET_QK_DIM]
            q_ref[:, n0 + d0:n0 + d0 + half] = (t1 * cos - t2 * sin).astype(BF16)
            q_ref[:, n0 + d0 + half:n0 + d0 + RET_QK_DIM] = (t1 * sin + t2 * cos).astype(BF16)
    cos_t, sin_t = cost_ref[...], sint_ref[...]
    k_scale = RET_QK_DIM ** -0.5
    for r0 in range(0, D_MODEL, RET_QK_DIM):
        acc = _dot_nt(wkt_ref[r0:r0 + RET_QK_DIM, :], h)
        t1 = acc[:half]
        t2 = acc[half:]
        kt_ref[r0:r0 + half, :] = ((t1 * cos_t - t2 * sin_t) * k_scale).astype(BF16)
        kt_ref[r0 + half:r0 + RET_QK_DIM, :] = ((t1 * sin_t + t2 * cos_t) * k_scale).astype(BF16)
    for n0 in range(0, 2 * D_MODEL, tn):
        v_ref[:, n0:n0 + tn] = _dot(h, wvg_ref[:, n0:n0 + tn]).astype(BF16)
    for n0 in range(0, 2 * D_MODEL, tn):
        gate = _dot(h, wvg_ref[:, 2 * D_MODEL + n0:2 * D_MODEL + n0 + tn])
        sg_ref[:, n0:n0 + tn] = (gate * jax.nn.sigmoid(gate)).astype(BF16)


def _ret_proj(x2d, g, w_q, w_kt, w_vg, cos, sin, seq):
    t = x2d.shape[0]
    tm = ROW_TILE
    tiles_per_seq = seq // tm
    half = RET_QK_DIM // 2
    row = lambda i: (i, 0)
    const = lambda i: (0, 0)
    return pl.pallas_call(
        _ret_proj_kernel,
        grid=(t // tm,),
        in_specs=[
            pl.BlockSpec((tm, D_MODEL), row),
            pl.BlockSpec((1, D_MODEL), const),
            pl.BlockSpec((D_MODEL, D_MODEL), const),
            pl.BlockSpec((D_MODEL, D_MODEL), const),
            pl.BlockSpec((D_MODEL, 4 * D_MODEL), const),
            pl.BlockSpec((tm, half), lambda i: (i % tiles_per_seq, 0)),
            pl.BlockSpec((tm, half), lambda i: (i % tiles_per_seq, 0)),
            pl.BlockSpec((half, tm), lambda i: (0, i % tiles_per_seq)),
            pl.BlockSpec((half, tm), lambda i: (0, i % tiles_per_seq)),
        ],
        out_specs=[
            pl.BlockSpec((tm, D_MODEL), row),
            pl.BlockSpec((None, D_MODEL, tm), lambda i: (i, 0, 0)),
            pl.BlockSpec((tm, 2 * D_MODEL), row),
            pl.BlockSpec((tm, 2 * D_MODEL), row),
        ],
        out_shape=[
            jax.ShapeDtypeStruct((t, D_MODEL), BF16),
            jax.ShapeDtypeStruct((t // tm, D_MODEL, tm), BF16),
            jax.ShapeDtypeStruct((t, 2 * D_MODEL), BF16),
            jax.ShapeDtypeStruct((t, 2 * D_MODEL), BF16),
        ],
        compiler_params=_params("arbitrary"),
        name="ret_proj",
    )(x2d, g, w_q, w_kt, w_vg, cos, sin, cos.T, sin.T)


def _retention_kernel(q_ref, kt_ref, v_ref, sg_ref, w_ref, qd_ref, kd_ref, o_ref, state_ref, *, block_decay):
    @pl.when(pl.program_id(1) == 0)
    def _():
        state_ref[...] = jnp.zeros_like(state_ref)

    for hd in range(RET_HEADS):
        qk_cols = slice(hd * RET_QK_DIM, (hd + 1) * RET_QK_DIM)
        v_cols = slice(hd * RET_V_DIM, (hd + 1) * RET_V_DIM)
        q = q_ref[:, qk_cols]
        kt = kt_ref[qk_cols, :]
        v = v_ref[:, v_cols]
        state = state_ref[hd]
        p = (_dot(q, kt) * w_ref[hd]).astype(BF16)
        q_dec = (q.astype(F32) * qd_ref[hd]).astype(BF16)
        y = _dot(jnp.concatenate([p, q_dec], axis=1), jnp.concatenate([v, state.astype(BF16)], axis=0))
        kt_dec = (kt.astype(F32) * kd_ref[hd]).astype(BF16)
        state_ref[hd] = state * block_decay[hd] + _dot(kt_dec, v)

        mu = jnp.mean(y, axis=-1, keepdims=True)
        yc = y - mu
        var = jnp.mean(yc * yc, axis=-1, keepdims=True)
        gated = sg_ref[:, v_cols].astype(F32) * (yc * lax.rsqrt(var + EPS))
        o_ref[:, v_cols] = gated.astype(o_ref.dtype)


def _retention_tables():
    r = RET_BLOCK
    gamma = 1.0 - 2.0 ** (-5.0 - np.arange(RET_HEADS, dtype=np.float64))
    log_gamma = jnp.log(jnp.asarray(gamma, dtype=F32))
    pos = np.arange(r)
    diff = (pos[:, None] - pos[None, :]).astype(np.float32)
    same = (pos[:, None] // CHUNK) == (pos[None, :] // CHUNK)
    earlier = (pos[None, :] // CHUNK) < (pos[:, None] // CHUNK)
    expo = np.where(same, np.abs(diff), diff)
    w = jnp.where(jnp.asarray(same | earlier)[None],
                  jnp.exp(log_gamma[:, None, None] * jnp.asarray(expo)[None]), 0.0)
    idx = jnp.arange(r, dtype=F32)
    qd = jnp.exp(log_gamma[:, None] * (idx[None, :] + 1.0))
    qd = jnp.broadcast_to(qd[..., None], (RET_HEADS, r, RET_QK_DIM))
    kd = jnp.exp(log_gamma[:, None] * (r - 1.0 - idx[None, :]))[:, None, :]
    block_decay = tuple(float(g) ** r for g in gamma.astype(np.float32).astype(np.float64))
    return w, qd, kd, block_decay


def _retention(q, kt, v, sg, batch, seq):
    t = q.shape[0]
    r = RET_BLOCK
    nb = seq // r
    per_slab = kt.shape[2] // r
    w, qd, kd, block_decay = _retention_tables()
    blk = lambda b, n: (b * nb + n, 0)
    const = lambda b, n: (0, 0, 0)
    return pl.pallas_call(
        functools.partial(_retention_kernel, block_decay=block_decay),
        grid=(batch, nb),
        in_specs=[
            pl.BlockSpec((r, D_MODEL), blk),
            pl.BlockSpec((None, D_MODEL, r), lambda b, n: ((b * nb + n) // per_slab, 0, (b * nb + n) % per_slab)),
            pl.BlockSpec((r, 2 * D_MODEL), blk),
            pl.BlockSpec((r, 2 * D_MODEL), blk),
            pl.BlockSpec((RET_HEADS, r, r), const),
            pl.BlockSpec((RET_HEADS, r, RET_QK_DIM), const),
            pl.BlockSpec((RET_HEADS, 1, r), const),
        ],
        out_specs=pl.BlockSpec((r, 2 * D_MODEL), blk),
        out_shape=jax.ShapeDtypeStruct((t, 2 * D_MODEL), BF16),
        scratch_shapes=[pltpu.VMEM((RET_HEADS, RET_QK_DIM, RET_V_DIM), F32)],
        compiler_params=_params("arbitrary", "arbitrary"),
        name="retention",
    )(q, kt, v, sg, w, qd, kd)


def kernel(x, norm_mix, norm_ffn, fox_w_in, fox_b_f, fox_w_out, ret_w_in, ret_w_out,
           ffn_w_in, ffn_w_out, final_norm):
    batch, seq, d = x.shape
    t = batch * seq
    x2d = x.reshape(t, d)
    row = lambda a: a.reshape(1, -1)

    w_in = fox_w_in[0]
    w_k = w_in[:, D_MODEL:2 * D_MODEL].astype(BF16)
    w_qvt = jnp.concatenate([w_in[:, :D_MODEL], w_in[:, 2 * D_MODEL:3 * D_MODEL]], axis=1).T.astype(BF16)
    pad = LANES - C_PARTS * FOX_HEADS
    wf = jnp.pad(jnp.tile(w_in[:, 3 * D_MODEL:], (1, C_PARTS)), ((0, 0), (0, pad))).astype(BF16)
    bf = jnp.pad(jnp.tile(fox_b_f[0], C_PARTS), (0, pad)).reshape(1, LANES)
    k, qvt, ca = _fox_proj(x2d, row(norm_mix[0]), w_k, w_qvt, wf, bf, seq)
    attn = _fox_attn(k, ca, qvt, batch, seq)
    x2d = _ffn(attn, fox_w_out[0].astype(BF16), x2d, row(norm_ffn[0]),
               ffn_w_in[0][:, :D_FF].astype(BF16), ffn_w_in[0][:, D_FF:].astype(BF16),
               ffn_w_out[0].astype(BF16), row(final_norm), False, "ffn0")

    half = RET_QK_DIM // 2
    inv = ROPE_BASE ** (-jnp.arange(half, dtype=F32) / half)
    ang = jnp.arange(seq, dtype=F32)[:, None] * inv[None, :]
    w_in = ret_w_in[0]
    q, kt, v, sg = _ret_proj(x2d, row(norm_mix[1]), w_in[:, :D_MODEL].astype(BF16),
                             w_in[:, D_MODEL:2 * D_MODEL].T.astype(BF16), w_in[:, 2 * D_MODEL:].astype(BF16),
                             jnp.cos(ang), jnp.sin(ang), seq)
    gated = _retention(q, kt, v, sg, batch, seq)
    x2d = _ffn(gated, ret_w_out[0].astype(BF16), x2d, row(norm_ffn[1]),
               ffn_w_in[1][:, :D_FF].astype(BF16), ffn_w_in[1][:, D_FF:].astype(BF16),
               ffn_w_out[1].astype(BF16), row(final_norm), True, "ffn1")
    return x2d.reshape(batch, seq, d)
```

```python
import functools
import math

import jax
import jax.numpy as jnp
import numpy as np
from jax import lax
from jax.experimental import pallas as pl
from jax.experimental.pallas import tpu as pltpu

D_MODEL = 1024
CHUNK = 64
FOX_HEADS = 8
FOX_HEAD_DIM = D_MODEL // FOX_HEADS
RET_HEADS = 4
RET_QK_DIM = D_MODEL // RET_HEADS
RET_V_DIM = 2 * D_MODEL // RET_HEADS
D_FF = -(-8 * D_MODEL // (3 * 256)) * 256
ROPE_BASE = 10000.0
EPS = 1e-6
LOG2E = math.log2(math.e)

F32 = jnp.float32
BF16 = jnp.bfloat16

VMEM_LIMIT_BYTES = 56 * 1024 * 1024
LANES = 128

ROW_TILE = 512
ATTN_Q_TILE = 2 * ROW_TILE
ATTN_K_TILE = ATTN_Q_TILE // 4
RET_BLOCK = 256
FFN_CHUNKS = ((0, 1024), (1024, 1024), (2048, 768))
C_PARTS = 3
ONES_ROWS = 16


def _params(*semantics):
    return pltpu.CompilerParams(dimension_semantics=semantics, vmem_limit_bytes=VMEM_LIMIT_BYTES)


def _rmsnorm(x, g):
    return x * lax.rsqrt(jnp.mean(x * x, axis=-1, keepdims=True) + EPS) * g


def _dot(a, b):
    return jnp.dot(a, b, preferred_element_type=F32)


def _dot_nt(a, b):
    return lax.dot_general(a, b, (((1,), (1,)), ((), ())), preferred_element_type=F32)


def _dot_tn(a, b):
    return lax.dot_general(a, b, (((0,), (0,)), ((), ())), preferred_element_type=F32)


def _cumsum_rows(v):
    n = v.shape[0]
    row = lax.broadcasted_iota(jnp.int32, v.shape, 0)
    shift = 1
    while shift < n:
        rolled = pltpu.roll(v, shift, axis=0)
        v = v + jnp.where(row >= shift, rolled, 0.0)
        shift *= 2
    return v


def _fox_proj_kernel(x_ref, g_ref, wk_ref, wqvt_ref, wf_ref, bf_ref, k_ref, qvt_ref, ca_ref, carry_ref,
                     *, tiles_per_seq):
    i = pl.program_id(0)
    tm = x_ref.shape[0]
    h = _rmsnorm(x_ref[...], g_ref[...]).astype(BF16)
    q_scale = FOX_HEAD_DIM ** -0.5 * LOG2E
    tn = 512
    for n0 in range(0, D_MODEL, tn):
        acc = _dot(h, wk_ref[:, n0:n0 + tn]).astype(BF16)
        for d0 in range(0, tn, FOX_HEAD_DIM):
            k_ref[(n0 + d0) // FOX_HEAD_DIM] = acc[:, d0:d0 + FOX_HEAD_DIM]
    for r0 in range(0, 2 * D_MODEL, tn):
        acc = _dot_nt(wqvt_ref[r0:r0 + tn, :], h)
        if r0 < D_MODEL:
            acc = acc * q_scale
        qvt_ref[r0:r0 + tn, :] = acc.astype(BF16)

    logits = _dot(h, wf_ref[...]) + bf_ref[...]
    log_f = jnp.minimum(logits, 0.0) - jnp.log1p(jnp.exp(-jnp.abs(logits)))

    @pl.when(i % tiles_per_seq == 0)
    def _():
        carry_ref[...] = jnp.zeros_like(carry_ref)

    c = _cumsum_rows(log_f) + carry_ref[...]
    carry_ref[...] = c[tm - 1:tm, :]
    neg_c = c * (-LOG2E)
    hi = neg_c.astype(BF16)
    rest = neg_c - hi.astype(F32)
    mid = rest.astype(BF16)
    lo = (rest - mid.astype(F32)).astype(BF16)
    lane = lax.broadcasted_iota(jnp.int32, neg_c.shape, 1)
    ca_ref[...] = jnp.where(lane < FOX_HEADS, hi, jnp.where(lane < 2 * FOX_HEADS, mid, lo))


def _fox_proj(x2d, g, w_k, w_qvt, wf, bf, seq):
    t = x2d.shape[0]
    tm = ROW_TILE
    return pl.pallas_call(
        functools.partial(_fox_proj_kernel, tiles_per_seq=seq // tm),
        grid=(t // tm,),
        in_specs=[
            pl.BlockSpec((tm, D_MODEL), lambda i: (i, 0)),
            pl.BlockSpec((1, D_MODEL), lambda i: (0, 0)),
            pl.BlockSpec((D_MODEL, D_MODEL), lambda i: (0, 0)),
            pl.BlockSpec((2 * D_MODEL, D_MODEL), lambda i: (0, 0)),
            pl.BlockSpec((D_MODEL, LANES), lambda i: (0, 0)),
            pl.BlockSpec((1, LANES), lambda i: (0, 0)),
        ],
        out_specs=[
            pl.BlockSpec((FOX_HEADS, tm, FOX_HEAD_DIM), lambda i: (0, i, 0)),
            pl.BlockSpec((None, 2 * D_MODEL, tm), lambda i: (i, 0, 0)),
            pl.BlockSpec((tm, LANES), lambda i: (i, 0)),
        ],
        out_shape=[
            jax.ShapeDtypeStruct((FOX_HEADS, t, FOX_HEAD_DIM), BF16),
            jax.ShapeDtypeStruct((t // tm, 2 * D_MODEL, tm), BF16),
            jax.ShapeDtypeStruct((t, LANES), BF16),
        ],
        scratch_shapes=[pltpu.VMEM((1, LANES), F32)],
        compiler_params=_params("arbitrary"),
        name="fox_proj",
    )(x2d, g, w_k, w_qvt, wf, bf)


def _fox_attn_kernel(qt_ref, k_ref, ca_ref, vt_ref, o_ref, s_ref, smax_ref, m_ref, acc_ref):
    head = pl.program_id(1)
    tq = ATTN_Q_TILE
    tk = ATTN_K_TILE
    tv = vt_ref.shape[2]
    nq = (qt_ref.shape[0] * tv) // tq
    assert tq == 4 * tk and tq == 2 * tv

    piece = lax.broadcasted_iota(jnp.int32, (LANES, tq), 0)
    pick = jnp.logical_and(piece % FOX_HEADS == head, piece < C_PARTS * FOX_HEADS)
    ones_rows = jnp.where(pick, 1.0, 0.0).astype(BF16)

    def queries(i):
        return jnp.concatenate([jnp.concatenate([qt_ref[2 * i], qt_ref[2 * i + 1]], axis=1), ones_rows], axis=0)

    def scores(qt_ext, row0, c0=0):
        k_ext = jnp.concatenate([k_ref[pl.ds(row0, tk), :], ca_ref[pl.ds(row0, tk), :]], axis=1)
        return _dot(k_ext, qt_ext[:, c0:])

    def update(tiles, vt, c0=0, key_offsets=None, tile_maxes=None):
        if key_offsets is not None:
            key = lax.broadcasted_iota(jnp.int32, tiles[0].shape, 0)
            qry = lax.broadcasted_iota(jnp.int32, tiles[0].shape, 1)
            tiles = [jnp.where(qry >= key + off, s, -jnp.inf) for s, off in zip(tiles, key_offsets)]
        m_old = m_ref[:, c0:]
        m_new = m_old
        for idx, s in enumerate(tiles):
            tile_max = jnp.max(s, axis=0, keepdims=True) if tile_maxes is None else tile_maxes[idx]
            m_new = jnp.maximum(m_new, tile_max)
        alpha = jnp.exp2(m_old - m_new)
        p = [jnp.exp2(s - m_new).astype(BF16) for s in tiles]
        p = p[0] if len(p) == 1 else jnp.concatenate(p, axis=0)
        vt_ext = jnp.concatenate([vt, jnp.ones((ONES_ROWS, vt.shape[1]), BF16)], axis=0)
        acc_ref[:, c0:] = alpha * acc_ref[:, c0:] + _dot(vt_ext, p)
        m_ref[:, c0:] = m_new

    def stash(slot, qt_ext, row0):
        s = scores(qt_ext, row0)
        s_ref[slot] = s
        smax_ref[slot] = jnp.max(s, axis=0, keepdims=True)

    def query_tile(i, carry):
        qt_ext = queries(i)
        m_ref[...] = jnp.full_like(m_ref, -jnp.inf)
        acc_ref[...] = jnp.zeros_like(acc_ref)

        def body(j, carry):
            base = pl.multiple_of(j * tq, tq)
            stash(2, qt_ext, base + 2 * tk)
            stash(3, qt_ext, base + 3 * tk)
            update([s_ref[0], s_ref[1]], vt_ref[2 * j], tile_maxes=[smax_ref[0], smax_ref[1]])
            stash(0, qt_ext, base + tq)
            stash(1, qt_ext, base + tq + tk)
            update([s_ref[2], s_ref[3]], vt_ref[2 * j + 1], tile_maxes=[smax_ref[2], smax_ref[3]])
            return carry

        lax.fori_loop(0, i, body, 0)
        base = pl.multiple_of(i * tq, tq)
        s2 = scores(qt_ext, base + 2 * tk, 2 * tk)
        s3 = scores(qt_ext, base + 3 * tk, 2 * tk)
        update([s_ref[0], s_ref[1]], vt_ref[2 * i], 0, key_offsets=[0, tk])
        qt_next = queries(jnp.minimum(i + 1, nq - 1))
        stash(0, qt_next, 0)
        stash(1, qt_next, tk)
        update([s2, s3], vt_ref[2 * i + 1], 2 * tk, key_offsets=[0, tk])
        acc = acc_ref[...]
        out = acc[:FOX_HEAD_DIM] / acc[FOX_HEAD_DIM:FOX_HEAD_DIM + 1]
        o_ref[pl.ds(base, tq), :] = jnp.transpose(out).astype(o_ref.dtype)
        return carry

    first = queries(0)
    stash(0, first, 0)
    stash(1, first, tk)
    lax.fori_loop(0, nq, query_tile, 0)


def _fox_attn(k, ca, qvt, batch, seq):
    t = k.shape[1]
    tq = ATTN_Q_TILE
    tk = ATTN_K_TILE
    tv = qvt.shape[2]
    hb = D_MODEL // FOX_HEAD_DIM
    return pl.pallas_call(
        _fox_attn_kernel,
        grid=(batch, FOX_HEADS),
        in_specs=[
            pl.BlockSpec((seq // tv, FOX_HEAD_DIM, tv), lambda b, h: (b, h, 0)),
            pl.BlockSpec((None, seq, FOX_HEAD_DIM), lambda b, h: (h, b, 0)),
            pl.BlockSpec((seq, LANES), lambda b, h: (b, 0)),
            pl.BlockSpec((seq // tv, FOX_HEAD_DIM, tv), lambda b, h: (b, hb + h, 0)),
        ],
        out_specs=pl.BlockSpec((None, seq, FOX_HEAD_DIM), lambda b, h: (h, b, 0)),
        out_shape=jax.ShapeDtypeStruct((FOX_HEADS, t, FOX_HEAD_DIM), BF16),
        scratch_shapes=[
            pltpu.VMEM((4, tk, tq), F32),
            pltpu.VMEM((4, 1, tq), F32),
            pltpu.VMEM((1, tq), F32),
            pltpu.VMEM((FOX_HEAD_DIM + ONES_ROWS, tq), F32),
        ],
        compiler_params=_params("arbitrary", "arbitrary"),
        name="fox_attn",
    )(qvt, k, ca, qvt)


def _ffn_kernel(a_ref, wp_ref, x_ref, g_ref, wi_ref, wo_ref, fg_ref, o_ref, *, final_norm):
    if len(a_ref.shape) == 3:
        mixed = jnp.concatenate([a_ref[hd] for hd in range(a_ref.shape[0])], axis=1)
    else:
        mixed = a_ref[...]
    x = x_ref[...] + _dot(mixed, wp_ref[...])
    h = _rmsnorm(x, g_ref[...]).astype(BF16)
    out = x
    for c0, width in FFN_CHUNKS:
        gate = _dot(h, wi_ref[:, c0:c0 + width])
        up = _dot(h, wi_ref[:, D_FF + c0:D_FF + c0 + width])
        act = (gate * jax.nn.sigmoid(gate) * up).astype(BF16)
        out = out + _dot(act, wo_ref[c0:c0 + width, :])
    if final_norm:
        out = _rmsnorm(out, fg_ref[...])
    o_ref[...] = out


def _ffn(mixed, w_proj, x2d, g, w_in_all, w_out_all, layer, final_g, final_norm, name):
    t = x2d.shape[0]
    tm = ROW_TILE
    if mixed.ndim == 3:
        mixed_spec = pl.BlockSpec((mixed.shape[0], tm, mixed.shape[2]), lambda i: (0, i, 0))
    else:
        mixed_spec = pl.BlockSpec((tm, mixed.shape[1]), lambda i: (i, 0))
    return pl.pallas_call(
        functools.partial(_ffn_kernel, final_norm=final_norm),
        grid=(t // tm,),
        in_specs=[
            mixed_spec,
            pl.BlockSpec(w_proj.shape, lambda i: (0, 0)),
            pl.BlockSpec((tm, D_MODEL), lambda i: (i, 0)),
            pl.BlockSpec((1, D_MODEL), lambda i: (0, 0)),
            pl.BlockSpec((None, D_MODEL, 2 * D_FF), lambda i: (layer, 0, 0), pipeline_mode=pl.Buffered(1)),
            pl.BlockSpec((None, D_FF, D_MODEL), lambda i: (layer, 0, 0), pipeline_mode=pl.Buffered(1)),
            pl.BlockSpec((1, D_MODEL), lambda i: (0, 0)),
        ],
        out_specs=pl.BlockSpec((tm, D_MODEL), lambda i: (i, 0)),
        out_shape=jax.ShapeDtypeStruct((t, D_MODEL), F32),
        compiler_params=_params("arbitrary"),
        name=name,
    )(mixed, w_proj, x2d, g, w_in_all, w_out_all, final_g)


def _ret_proj_kernel(x_ref, g_ref, wq_ref, wkt_ref, wv_ref, wg_ref, cos_ref, sin_ref, cost_ref, sint_ref,
                     q_ref, kt_ref, v_ref, sg_ref):
    h = _rmsnorm(x_ref[...], g_ref[...]).astype(BF16)
    half = RET_QK_DIM // 2
    tn = 512
    cos, sin = cos_ref[...], sin_ref[...]
    for n0 in range(0, D_MODEL, tn):
        acc = _dot(h, wq_ref[:, n0:n0 + tn])
        for d0 in range(0, tn, RET_QK_DIM):
            t1 = acc[:, d0:d0 + half]
            t2 = acc[:, d0 + half:d0 + RET_QK_DIM]
            q_ref[:, n0 + d0:n0 + d0 + half] = (t1 * cos - t2 * sin).astype(BF16)
            q_ref[:, n0 + d0 + half:n0 + d0 + RET_QK_DIM] = (t1 * sin + t2 * cos).astype(BF16)
    cos_t, sin_t = cost_ref[...], sint_ref[...]
    k_scale = RET_QK_DIM ** -0.5
    for r0 in range(0, D_MODEL, RET_QK_DIM):
        acc = _dot_nt(wkt_ref[r0:r0 + RET_QK_DIM, :], h)
        t1 = acc[:half]
        t2 = acc[half:]
        kt_ref[r0:r0 + half, :] = ((t1 * cos_t - t2 * sin_t) * k_scale).astype(BF16)
        kt_ref[r0 + half:r0 + RET_QK_DIM, :] = ((t1 * sin_t + t2 * cos_t) * k_scale).astype(BF16)
    for n0 in range(0, 2 * D_MODEL, tn):
        v_ref[:, n0:n0 + tn] = _dot(h, wv_ref[:, n0:n0 + tn]).astype(BF16)
    for n0 in range(0, 2 * D_MODEL, tn):
        gate = _dot(h, wg_ref[:, n0:n0 + tn])
        sg_ref[:, n0:n0 + tn] = (gate * jax.nn.sigmoid(gate)).astype(BF16)


def _ret_proj(x2d, g, w, w_kt, cos, sin, seq):
    t = x2d.shape[0]
    tm = ROW_TILE
    tiles_per_seq = seq // tm
    half = RET_QK_DIM // 2
    row = lambda i: (i, 0)
    const = lambda i: (0, 0)
    return pl.pallas_call(
        _ret_proj_kernel,
        grid=(t // tm,),
        in_specs=[
            pl.BlockSpec((tm, D_MODEL), row),
            pl.BlockSpec((1, D_MODEL), const),
            pl.BlockSpec((D_MODEL, D_MODEL), const, pipeline_mode=pl.Buffered(1)),
            pl.BlockSpec((D_MODEL, D_MODEL), const),
            pl.BlockSpec((D_MODEL, 2 * D_MODEL), lambda i: (0, 1), pipeline_mode=pl.Buffered(1)),
            pl.BlockSpec((D_MODEL, 2 * D_MODEL), lambda i: (0, 2), pipeline_mode=pl.Buffered(1)),
            pl.BlockSpec((tm, half), lambda i: (i % tiles_per_seq, 0)),
            pl.BlockSpec((tm, half), lambda i: (i % tiles_per_seq, 0)),
            pl.BlockSpec((half, tm), lambda i: (0, i % tiles_per_seq)),
            pl.BlockSpec((half, tm), lambda i: (0, i % tiles_per_seq)),
        ],
        out_specs=[
            pl.BlockSpec((tm, D_MODEL), row),
            pl.BlockSpec((None, D_MODEL, tm), lambda i: (i, 0, 0)),
            pl.BlockSpec((tm, 2 * D_MODEL), row),
            pl.BlockSpec((tm, 2 * D_MODEL), row),
        ],
        out_shape=[
            jax.ShapeDtypeStruct((t, D_MODEL), BF16),
            jax.ShapeDtypeStruct((t // tm, D_MODEL, tm), BF16),
            jax.ShapeDtypeStruct((t, 2 * D_MODEL), BF16),
            jax.ShapeDtypeStruct((t, 2 * D_MODEL), BF16),
        ],
        compiler_params=_params("arbitrary"),
        name="ret_proj",
    )(x2d, g, w, w_kt, w, w, cos, sin, cos.T, sin.T)


def _retention_kernel(q_ref, kt_ref, v_ref, sg_ref, w_ref, qd_ref, kd_ref, o_ref, state_ref, *, block_decay):
    @pl.when(pl.program_id(1) == 0)
    def _():
        state_ref[...] = jnp.zeros_like(state_ref)

    for hd in range(RET_HEADS):
        qk_cols = slice(hd * RET_QK_DIM, (hd + 1) * RET_QK_DIM)
        v_cols = slice(hd * RET_V_DIM, (hd + 1) * RET_V_DIM)
        q = q_ref[:, qk_cols]
        kt = kt_ref[qk_cols, :]
        v = v_ref[:, v_cols]
        state = state_ref[hd]
        p = (_dot(q, kt) * w_ref[hd]).astype(BF16)
        q_dec = (q.astype(F32) * qd_ref[hd]).astype(BF16)
        y = _dot(jnp.concatenate([p, q_dec], axis=1), jnp.concatenate([v, state.astype(BF16)], axis=0))
        kt_dec = (kt.astype(F32) * kd_ref[hd]).astype(BF16)
        state_ref[hd] = state * block_decay[hd] + _dot(kt_dec, v)

        mu = jnp.mean(y, axis=-1, keepdims=True)
        yc = y - mu
        var = jnp.mean(yc * yc, axis=-1, keepdims=True)
        gated = sg_ref[:, v_cols].astype(F32) * (yc * lax.rsqrt(var + EPS))
        o_ref[:, v_cols] = gated.astype(o_ref.dtype)


def _retention_tables():
    r = RET_BLOCK
    gamma = 1.0 - 2.0 ** (-5.0 - np.arange(RET_HEADS, dtype=np.float64))
    log_gamma = jnp.log(jnp.asarray(gamma, dtype=F32))
    pos = np.arange(r)
    diff = (pos[:, None] - pos[None, :]).astype(np.float32)
    same = (pos[:, None] // CHUNK) == (pos[None, :] // CHUNK)
    earlier = (pos[None, :] // CHUNK) < (pos[:, None] // CHUNK)
    expo = np.where(same, np.abs(diff), diff)
    w = jnp.where(jnp.asarray(same | earlier)[None],
                  jnp.exp(log_gamma[:, None, None] * jnp.asarray(expo)[None]), 0.0)
    idx = jnp.arange(r, dtype=F32)
    qd = jnp.exp(log_gamma[:, None] * (idx[None, :] + 1.0))
    qd = jnp.broadcast_to(qd[..., None], (RET_HEADS, r, RET_QK_DIM))
    kd = jnp.exp(log_gamma[:, None] * (r - 1.0 - idx[None, :]))[:, None, :]
    block_decay = tuple(float(g) ** r for g in gamma.astype(np.float32).astype(np.float64))
    return w, qd, kd, block_decay


def _retention(q, kt, v, sg, batch, seq):
    t = q.shape[0]
    r = RET_BLOCK
    nb = seq // r
    per_slab = kt.shape[2] // r
    w, qd, kd, block_decay = _retention_tables()
    blk = lambda b, n: (b * nb + n, 0)
    const = lambda b, n: (0, 0, 0)
    return pl.pallas_call(
        functools.partial(_retention_kernel, block_decay=block_decay),
        grid=(batch, nb),
        in_specs=[
            pl.BlockSpec((r, D_MODEL), blk),
            pl.BlockSpec((None, D_MODEL, r), lambda b, n: ((b * nb + n) // per_slab, 0, (b * nb + n) % per_slab)),
            pl.BlockSpec((r, 2 * D_MODEL), blk),
            pl.BlockSpec((r, 2 * D_MODEL), blk),
            pl.BlockSpec((RET_HEADS, r, r), const),
            pl.BlockSpec((RET_HEADS, r, RET_QK_DIM), const),
            pl.BlockSpec((RET_HEADS, 1, r), const),
        ],
        out_specs=pl.BlockSpec((r, 2 * D_MODEL), blk),
        out_shape=jax.ShapeDtypeStruct((t, 2 * D_MODEL), BF16),
        scratch_shapes=[pltpu.VMEM((RET_HEADS, RET_QK_DIM, RET_V_DIM), F32)],
        compiler_params=_params("arbitrary", "arbitrary"),
        name="retention",
    )(q, kt, v, sg, w, qd, kd)


def kernel(x, norm_mix, norm_ffn, fox_w_in, fox_b_f, fox_w_out, ret_w_in, ret_w_out,
           ffn_w_in, ffn_w_out, final_norm):
    batch, seq, d = x.shape
    t = batch * seq
    x2d = x.reshape(t, d)
    row = lambda a: a.reshape(1, -1)

    w_in = fox_w_in[0]
    w_k = w_in[:, D_MODEL:2 * D_MODEL].astype(BF16)
    w_qvt = jnp.concatenate([w_in[:, :D_MODEL], w_in[:, 2 * D_MODEL:3 * D_MODEL]], axis=1).T.astype(BF16)
    pad = LANES - C_PARTS * FOX_HEADS
    wf = jnp.pad(jnp.tile(w_in[:, 3 * D_MODEL:], (1, C_PARTS)), ((0, 0), (0, pad))).astype(BF16)
    bf = jnp.pad(jnp.tile(fox_b_f[0], C_PARTS), (0, pad)).reshape(1, LANES)
    k, qvt, ca = _fox_proj(x2d, row(norm_mix[0]), w_k, w_qvt, wf, bf, seq)
    attn = _fox_attn(k, ca, qvt, batch, seq)
    ffn_in = ffn_w_in.astype(BF16)
    ffn_out = ffn_w_out.astype(BF16)
    x2d = _ffn(attn, fox_w_out[0].astype(BF16), x2d, row(norm_ffn[0]), ffn_in, ffn_out, 0,
               row(final_norm), False, "ffn0")

    half = RET_QK_DIM // 2
    inv = ROPE_BASE ** (-jnp.arange(half, dtype=F32) / half)
    ang = jnp.arange(seq, dtype=F32)[:, None] * inv[None, :]
    w_in = ret_w_in[0]
    q, kt, v, sg = _ret_proj(x2d, row(norm_mix[1]), w_in.astype(BF16),
                             w_in[:, D_MODEL:2 * D_MODEL].T.astype(BF16), jnp.cos(ang), jnp.sin(ang), seq)
    gated = _retention(q, kt, v, sg, batch, seq)
    x2d = _ffn(gated, ret_w_out[0].astype(BF16), x2d, row(norm_ffn[1]), ffn_in, ffn_out, 1,
               row(final_norm), True, "ffn1")
    return x2d.reshape(batch, seq, d)
```

```python
import functools
import math

import jax
import jax.numpy as jnp
import numpy as np
from jax import lax
from jax.experimental import pallas as pl
from jax.experimental.pallas import tpu as pltpu

D_MODEL = 1024
CHUNK = 64
FOX_HEADS = 8
FOX_HEAD_DIM = D_MODEL // FOX_HEADS
RET_HEADS = 4
RET_QK_DIM = D_MODEL // RET_HEADS
RET_V_DIM = 2 * D_MODEL // RET_HEADS
D_FF = -(-8 * D_MODEL // (3 * 256)) * 256
ROPE_BASE = 10000.0
EPS = 1e-6
LOG2E = math.log2(math.e)

F32 = jnp.float32
BF16 = jnp.bfloat16

VMEM_LIMIT_BYTES = 56 * 1024 * 1024
LANES = 128

ROW_TILE = 512
ATTN_Q_TILE = 2 * ROW_TILE
ATTN_K_TILE = ATTN_Q_TILE // 2
RET_BLOCK = 256
FFN_CHUNKS = ((0, 1024), (1024, 1024), (2048, 768))
C_PARTS = 3
ONES_ROWS = 16


def _params(*semantics):
    return pltpu.CompilerParams(dimension_semantics=semantics, vmem_limit_bytes=VMEM_LIMIT_BYTES)


def _rmsnorm(x, g):
    return x * lax.rsqrt(jnp.mean(x * x, axis=-1, keepdims=True) + EPS) * g


def _dot(a, b):
    return jnp.dot(a, b, preferred_element_type=F32)


def _dot_nt(a, b):
    return lax.dot_general(a, b, (((1,), (1,)), ((), ())), preferred_element_type=F32)


def _dot_tn(a, b):
    return lax.dot_general(a, b, (((0,), (0,)), ((), ())), preferred_element_type=F32)


def _cumsum_rows(v):
    n = v.shape[0]
    row = lax.broadcasted_iota(jnp.int32, v.shape, 0)
    shift = 1
    while shift < n:
        rolled = pltpu.roll(v, shift, axis=0)
        v = v + jnp.where(row >= shift, rolled, 0.0)
        shift *= 2
    return v


def _fox_proj_kernel(x_ref, g_ref, wk_ref, wqvt_ref, wf_ref, bf_ref, k_ref, qvt_ref, ca_ref, carry_ref,
                     *, tiles_per_seq):
    i = pl.program_id(0)
    tm = x_ref.shape[0]
    h = _rmsnorm(x_ref[...], g_ref[...]).astype(BF16)
    logits = _dot(h, wf_ref[...]) + bf_ref[...]
    log_f = jnp.minimum(logits, 0.0) - jnp.log1p(jnp.exp(-jnp.abs(logits)))
    carry = jnp.where(i % tiles_per_seq == 0, 0.0, carry_ref[...])
    c = _cumsum_rows(log_f) + carry
    carry_ref[...] = c[tm - 1:tm, :]
    neg_c = c * (-LOG2E)
    hi = neg_c.astype(BF16)
    rest = neg_c - hi.astype(F32)
    mid = rest.astype(BF16)
    lo = (rest - mid.astype(F32)).astype(BF16)
    lane = lax.broadcasted_iota(jnp.int32, neg_c.shape, 1)
    ca_ref[...] = jnp.where(lane < FOX_HEADS, hi, jnp.where(lane < 2 * FOX_HEADS, mid, lo))

    q_scale = FOX_HEAD_DIM ** -0.5 * LOG2E
    tn = 512
    for n0 in range(0, D_MODEL, tn):
        acc = _dot(h, wk_ref[:, n0:n0 + tn]).astype(BF16)
        for d0 in range(0, tn, FOX_HEAD_DIM):
            k_ref[(n0 + d0) // FOX_HEAD_DIM] = acc[:, d0:d0 + FOX_HEAD_DIM]
    for r0 in range(0, 2 * D_MODEL, tn):
        acc = _dot_nt(wqvt_ref[r0:r0 + tn, :], h)
        if r0 < D_MODEL:
            acc = acc * q_scale
        qvt_ref[r0:r0 + tn, :] = acc.astype(BF16)


def _fox_proj(x2d, g, w_k, w_qvt, wf, bf, seq):
    t = x2d.shape[0]
    tm = ROW_TILE
    return pl.pallas_call(
        functools.partial(_fox_proj_kernel, tiles_per_seq=seq // tm),
        grid=(t // tm,),
        in_specs=[
            pl.BlockSpec((tm, D_MODEL), lambda i: (i, 0)),
            pl.BlockSpec((1, D_MODEL), lambda i: (0, 0)),
            pl.BlockSpec((D_MODEL, D_MODEL), lambda i: (0, 0)),
            pl.BlockSpec((2 * D_MODEL, D_MODEL), lambda i: (0, 0)),
            pl.BlockSpec((D_MODEL, LANES), lambda i: (0, 0)),
            pl.BlockSpec((1, LANES), lambda i: (0, 0)),
        ],
        out_specs=[
            pl.BlockSpec((FOX_HEADS, tm, FOX_HEAD_DIM), lambda i: (0, i, 0)),
            pl.BlockSpec((None, 2 * D_MODEL, tm), lambda i: (i, 0, 0)),
            pl.BlockSpec((tm, LANES), lambda i: (i, 0)),
        ],
        out_shape=[
            jax.ShapeDtypeStruct((FOX_HEADS, t, FOX_HEAD_DIM), BF16),
            jax.ShapeDtypeStruct((t // tm, 2 * D_MODEL, tm), BF16),
            jax.ShapeDtypeStruct((t, LANES), BF16),
        ],
        scratch_shapes=[pltpu.VMEM((1, LANES), F32)],
        compiler_params=_params("arbitrary"),
        name="fox_proj",
    )(x2d, g, w_k, w_qvt, wf, bf)


def _fox_attn_kernel(qt_ref, k_ref, ca_ref, vt_ref, o_ref, s_ref, smax_ref, m_ref, acc_ref):
    head = pl.program_id(1)
    tq = ATTN_Q_TILE
    tk = ATTN_K_TILE
    tv = vt_ref.shape[2]
    nq = (qt_ref.shape[0] * tv) // tq
    assert tq == 2 * tk and tk == tv

    piece = lax.broadcasted_iota(jnp.int32, (LANES, tq), 0)
    pick = jnp.logical_and(piece % FOX_HEADS == head, piece < C_PARTS * FOX_HEADS)
    ones_rows = jnp.where(pick, 1.0, 0.0).astype(BF16)

    def queries(i):
        return jnp.concatenate([jnp.concatenate([qt_ref[2 * i], qt_ref[2 * i + 1]], axis=1), ones_rows], axis=0)

    def scores(qt_ext, row0, c0=0):
        k_ext = jnp.concatenate([k_ref[pl.ds(row0, tk), :], ca_ref[pl.ds(row0, tk), :]], axis=1)
        return _dot(k_ext, qt_ext[:, c0:])

    def update(s, vt, c0=0, diagonal=False, tile_max=None):
        if diagonal:
            key = lax.broadcasted_iota(jnp.int32, s.shape, 0)
            qry = lax.broadcasted_iota(jnp.int32, s.shape, 1)
            s = jnp.where(qry >= key, s, -jnp.inf)
        if tile_max is None:
            tile_max = jnp.max(s, axis=0, keepdims=True)
        m_old = m_ref[:, c0:]
        m_new = jnp.maximum(m_old, tile_max)
        alpha = jnp.exp2(m_old - m_new)
        p = jnp.exp2(s - m_new).astype(BF16)
        vt_ext = jnp.concatenate([vt, jnp.ones((ONES_ROWS, vt.shape[1]), BF16)], axis=0)
        acc_ref[:, c0:] = alpha * acc_ref[:, c0:] + _dot(vt_ext, p)
        m_ref[:, c0:] = m_new

    def stash(slot, qt_ext, row0):
        s = scores(qt_ext, row0)
        s_ref[slot] = s
        smax_ref[slot] = jnp.max(s, axis=0, keepdims=True)

    def query_tile(i, carry):
        qt_ext = queries(i)
        m_ref[...] = jnp.full_like(m_ref, -jnp.inf)
        acc_ref[...] = jnp.zeros_like(acc_ref)

        def body(j, carry):
            base = pl.multiple_of(j * tq, tq)
            stash(1, qt_ext, base + tk)
            update(s_ref[0], vt_ref[2 * j], tile_max=smax_ref[0])
            stash(0, qt_ext, base + tq)
            update(s_ref[1], vt_ref[2 * j + 1], tile_max=smax_ref[1])
            return carry

        lax.fori_loop(0, i, body, 0)
        base = pl.multiple_of(i * tq, tq)
        s_last = scores(qt_ext, base + tk, tk)
        update(s_ref[0], vt_ref[2 * i], 0, diagonal=True)
        qt_next = queries(jnp.minimum(i + 1, nq - 1))
        stash(0, qt_next, 0)
        update(s_last, vt_ref[2 * i + 1], tk, diagonal=True)
        acc = acc_ref[...]
        out = acc[:FOX_HEAD_DIM] / acc[FOX_HEAD_DIM:FOX_HEAD_DIM + 1]
        o_ref[pl.ds(base, tq), :] = jnp.transpose(out).astype(o_ref.dtype)
        return carry

    stash(0, queries(0), 0)
    lax.fori_loop(0, nq, query_tile, 0)


def _fox_attn(k, ca, qvt, batch, seq):
    t = k.shape[1]
    tq = ATTN_Q_TILE
    tk = ATTN_K_TILE
    tv = qvt.shape[2]
    hb = D_MODEL // FOX_HEAD_DIM
    return pl.pallas_call(
        _fox_attn_kernel,
        grid=(batch, FOX_HEADS),
        in_specs=[
            pl.BlockSpec((seq // tv, FOX_HEAD_DIM, tv), lambda b, h: (b, h, 0)),
            pl.BlockSpec((None, seq, FOX_HEAD_DIM), lambda b, h: (h, b, 0)),
            pl.BlockSpec((seq, LANES), lambda b, h: (b, 0)),
            pl.BlockSpec((seq // tv, FOX_HEAD_DIM, tv), lambda b, h: (b, hb + h, 0)),
        ],
        out_specs=pl.BlockSpec((None, seq, FOX_HEAD_DIM), lambda b, h: (h, b, 0)),
        out_shape=jax.ShapeDtypeStruct((FOX_HEADS, t, FOX_HEAD_DIM), BF16),
        scratch_shapes=[
            pltpu.VMEM((2, tk, tq), F32),
            pltpu.VMEM((2, 1, tq), F32),
            pltpu.VMEM((1, tq), F32),
            pltpu.VMEM((FOX_HEAD_DIM + ONES_ROWS, tq), F32),
        ],
        compiler_params=_params("arbitrary", "arbitrary"),
        name="fox_attn",
    )(qvt, k, ca, qvt)


def _ffn_kernel(a_ref, wp_ref, x_ref, g_ref, wi_ref, wo_ref, fg_ref, o_ref, *, final_norm):
    if len(a_ref.shape) == 3:
        mixed = jnp.concatenate([a_ref[hd] for hd in range(a_ref.shape[0])], axis=1)
    else:
        mixed = a_ref[...]
    x = x_ref[...] + _dot(mixed, wp_ref[...])
    h = _rmsnorm(x, g_ref[...]).astype(BF16)
    out = x
    for c0, width in FFN_CHUNKS:
        gate = _dot(h, wi_ref[:, c0:c0 + width])
        up = _dot(h, wi_ref[:, D_FF + c0:D_FF + c0 + width])
        act = (gate * jax.nn.sigmoid(gate) * up).astype(BF16)
        out = out + _dot(act, wo_ref[c0:c0 + width, :])
    if final_norm:
        out = _rmsnorm(out, fg_ref[...])
    o_ref[...] = out


def _ffn(mixed, w_proj, x2d, g, w_in_all, w_out_all, layer, final_g, final_norm, name):
    t = x2d.shape[0]
    tm = ROW_TILE
    if mixed.ndim == 3:
        mixed_spec = pl.BlockSpec((mixed.shape[0], tm, mixed.shape[2]), lambda i: (0, i, 0))
    else:
        mixed_spec = pl.BlockSpec((tm, mixed.shape[1]), lambda i: (i, 0))
    return pl.pallas_call(
        functools.partial(_ffn_kernel, final_norm=final_norm),
        grid=(t // tm,),
        in_specs=[
            mixed_spec,
            pl.BlockSpec(w_proj.shape, lambda i: (0, 0)),
            pl.BlockSpec((tm, D_MODEL), lambda i: (i, 0)),
            pl.BlockSpec((1, D_MODEL), lambda i: (0, 0)),
            pl.BlockSpec((None, D_MODEL, 2 * D_FF), lambda i: (layer, 0, 0), pipeline_mode=pl.Buffered(1)),
            pl.BlockSpec((None, D_FF, D_MODEL), lambda i: (layer, 0, 0), pipeline_mode=pl.Buffered(1)),
            pl.BlockSpec((1, D_MODEL), lambda i: (0, 0)),
        ],
        out_specs=pl.BlockSpec((tm, D_MODEL), lambda i: (i, 0)),
        out_shape=jax.ShapeDtypeStruct((t, D_MODEL), F32),
        compiler_params=_params("arbitrary"),
        name=name,
    )(mixed, w_proj, x2d, g, w_in_all, w_out_all, final_g)


def _ret_proj_kernel(x_ref, g_ref, wq_ref, wkt_ref, wv_ref, wg_ref, cos_ref, sin_ref, cost_ref, sint_ref,
                     q_ref, kt_ref, v_ref, sg_ref):
    h = _rmsnorm(x_ref[...], g_ref[...]).astype(BF16)
    half = RET_QK_DIM // 2
    tn = 512
    cos, sin = cos_ref[...], sin_ref[...]
    for n0 in range(0, D_MODEL, tn):
        acc = _dot(h, wq_ref[:, n0:n0 + tn])
        for d0 in range(0, tn, RET_QK_DIM):
            t1 = acc[:, d0:d0 + half]
            t2 = acc[:, d0 + half:d0 + RET_QK_DIM]
            q_ref[:, n0 + d0:n0 + d0 + half] = (t1 * cos - t2 * sin).astype(BF16)
            q_ref[:, n0 + d0 + half:n0 + d0 + RET_QK_DIM] = (t1 * sin + t2 * cos).astype(BF16)
    cos_t, sin_t = cost_ref[...], sint_ref[...]
    k_scale = RET_QK_DIM ** -0.5
    for r0 in range(0, D_MODEL, RET_QK_DIM):
        acc = _dot_nt(wkt_ref[r0:r0 + RET_QK_DIM, :], h)
        t1 = acc[:half]
        t2 = acc[half:]
        kt_ref[r0:r0 + half, :] = ((t1 * cos_t - t2 * sin_t) * k_scale).astype(BF16)
        kt_ref[r0 + half:r0 + RET_QK_DIM, :] = ((t1 * sin_t + t2 * cos_t) * k_scale).astype(BF16)
    for n0 in range(0, 2 * D_MODEL, tn):
        gate = _dot(h, wg_ref[:, n0:n0 + tn])
        sg_ref[:, n0:n0 + tn] = (gate * jax.nn.sigmoid(gate)).astype(BF16)
    for n0 in range(0, 2 * D_MODEL, tn):
        v_ref[:, n0:n0 + tn] = _dot(h, wv_ref[:, n0:n0 + tn]).astype(BF16)


def _ret_proj(x2d, g, w, w_kt, cos, sin, seq):
    t = x2d.shape[0]
    tm = ROW_TILE
    tiles_per_seq = seq // tm
    half = RET_QK_DIM // 2
    row = lambda i: (i, 0)
    const = lambda i: (0, 0)
    return pl.pallas_call(
        _ret_proj_kernel,
        grid=(t // tm,),
        in_specs=[
            pl.BlockSpec((tm, D_MODEL), row),
            pl.BlockSpec((1, D_MODEL), const),
            pl.BlockSpec((D_MODEL, D_MODEL), const, pipeline_mode=pl.Buffered(1)),
            pl.BlockSpec((D_MODEL, D_MODEL), const),
            pl.BlockSpec((D_MODEL, 2 * D_MODEL), lambda i: (0, 1), pipeline_mode=pl.Buffered(1)),
            pl.BlockSpec((D_MODEL, 2 * D_MODEL), lambda i: (0, 2), pipeline_mode=pl.Buffered(1)),
            pl.BlockSpec((tm, half), lambda i: (i % tiles_per_seq, 0)),
            pl.BlockSpec((tm, half), lambda i: (i % tiles_per_seq, 0)),
            pl.BlockSpec((half, tm), lambda i: (0, i % tiles_per_seq)),
            pl.BlockSpec((half, tm), lambda i: (0, i % tiles_per_seq)),
        ],
        out_specs=[
            pl.BlockSpec((tm, D_MODEL), row),
            pl.BlockSpec((None, D_MODEL, tm), lambda i: (i, 0, 0)),
            pl.BlockSpec((tm, 2 * D_MODEL), row),
            pl.BlockSpec((tm, 2 * D_MODEL), row),
        ],
        out_shape=[
            jax.ShapeDtypeStruct((t, D_MODEL), BF16),
            jax.ShapeDtypeStruct((t // tm, D_MODEL, tm), BF16),
            jax.ShapeDtypeStruct((t, 2 * D_MODEL), BF16),
            jax.ShapeDtypeStruct((t, 2 * D_MODEL), BF16),
        ],
        compiler_params=_params("arbitrary"),
        name="ret_proj",
    )(x2d, g, w, w_kt, w, w, cos, sin, cos.T, sin.T)


def _retention_kernel(q_ref, kt_ref, v_ref, sg_ref, w_ref, qd_ref, kd_ref, o_ref, state_ref, *, block_decay):
    @pl.when(pl.program_id(1) == 0)
    def _():
        state_ref[...] = jnp.zeros_like(state_ref)

    r = RET_BLOCK
    for blk in range(q_ref.shape[0] // r):
        rows = slice(blk * r, (blk + 1) * r)
        for hd in range(RET_HEADS):
            qk_cols = slice(hd * RET_QK_DIM, (hd + 1) * RET_QK_DIM)
            v_cols = slice(hd * RET_V_DIM, (hd + 1) * RET_V_DIM)
            q = q_ref[rows, qk_cols]
            kt = kt_ref[qk_cols, rows]
            v = v_ref[rows, v_cols]
            state = state_ref[hd]
            p = (_dot(q, kt) * w_ref[hd]).astype(BF16)
            q_dec = (q.astype(F32) * qd_ref[hd]).astype(BF16)
            y = _dot(jnp.concatenate([p, q_dec], axis=1), jnp.concatenate([v, state.astype(BF16)], axis=0))
            kt_dec = (kt.astype(F32) * kd_ref[hd]).astype(BF16)
            state_ref[hd] = state * block_decay[hd] + _dot(kt_dec, v)

            mu = jnp.mean(y, axis=-1, keepdims=True)
            yc = y - mu
            var = jnp.mean(yc * yc, axis=-1, keepdims=True)
            gated = sg_ref[rows, v_cols].astype(F32) * (yc * lax.rsqrt(var + EPS))
            o_ref[rows, v_cols] = gated.astype(o_ref.dtype)


def _retention_tables():
    r = RET_BLOCK
    gamma = 1.0 - 2.0 ** (-5.0 - np.arange(RET_HEADS, dtype=np.float64))
    log_gamma = jnp.log(jnp.asarray(gamma, dtype=F32))
    pos = np.arange(r)
    diff = (pos[:, None] - pos[None, :]).astype(np.float32)
    same = (pos[:, None] // CHUNK) == (pos[None, :] // CHUNK)
    earlier = (pos[None, :] // CHUNK) < (pos[:, None] // CHUNK)
    expo = np.where(same, np.abs(diff), diff)
    w = jnp.where(jnp.asarray(same | earlier)[None],
                  jnp.exp(log_gamma[:, None, None] * jnp.asarray(expo)[None]), 0.0)
    idx = jnp.arange(r, dtype=F32)
    qd = jnp.exp(log_gamma[:, None] * (idx[None, :] + 1.0))
    qd = jnp.broadcast_to(qd[..., None], (RET_HEADS, r, RET_QK_DIM))
    kd = jnp.exp(log_gamma[:, None] * (r - 1.0 - idx[None, :]))[:, None, :]
    block_decay = tuple(float(g) ** r for g in gamma.astype(np.float32).astype(np.float64))
    return w, qd, kd, block_decay


def _retention(q, kt, v, sg, batch, seq):
    t = q.shape[0]
    r = RET_BLOCK
    rows = kt.shape[2]
    nb = seq // rows
    w, qd, kd, block_decay = _retention_tables()
    blk = lambda b, n: (b * nb + n, 0)
    const = lambda b, n: (0, 0, 0)
    return pl.pallas_call(
        functools.partial(_retention_kernel, block_decay=block_decay),
        grid=(batch, nb),
        in_specs=[
            pl.BlockSpec((rows, D_MODEL), blk),
            pl.BlockSpec((None, D_MODEL, rows), lambda b, n: (b * nb + n, 0, 0)),
            pl.BlockSpec((rows, 2 * D_MODEL), blk),
            pl.BlockSpec((rows, 2 * D_MODEL), blk),
            pl.BlockSpec((RET_HEADS, r, r), const),
            pl.BlockSpec((RET_HEADS, r, RET_QK_DIM), const),
            pl.BlockSpec((RET_HEADS, 1, r), const),
        ],
        out_specs=pl.BlockSpec((rows, 2 * D_MODEL), blk),
        out_shape=jax.ShapeDtypeStruct((t, 2 * D_MODEL), BF16),
        scratch_shapes=[pltpu.VMEM((RET_HEADS, RET_QK_DIM, RET_V_DIM), F32)],
        compiler_params=_params("arbitrary", "arbitrary"),
        name="retention",
    )(q, kt, v, sg, w, qd, kd)


def kernel(x, norm_mix, norm_ffn, fox_w_in, fox_b_f, fox_w_out, ret_w_in, ret_w_out,
           ffn_w_in, ffn_w_out, final_norm):
    batch, seq, d = x.shape
    t = batch * seq
    x2d = x.reshape(t, d)
    row = lambda a: a.reshape(1, -1)

    w_in = fox_w_in[0]
    w_k = w_in[:, D_MODEL:2 * D_MODEL].astype(BF16)
    w_qvt = jnp.concatenate([w_in[:, :D_MODEL], w_in[:, 2 * D_MODEL:3 * D_MODEL]], axis=1).T.astype(BF16)
    pad = LANES - C_PARTS * FOX_HEADS
    wf = jnp.pad(jnp.tile(w_in[:, 3 * D_MODEL:], (1, C_PARTS)), ((0, 0), (0, pad))).astype(BF16)
    bf = jnp.pad(jnp.tile(fox_b_f[0], C_PARTS), (0, pad)).reshape(1, LANES)
    k, qvt, ca = _fox_proj(x2d, row(norm_mix[0]), w_k, w_qvt, wf, bf, seq)
    attn = _fox_attn(k, ca, qvt, batch, seq)
    ffn_in = ffn_w_in.astype(BF16)
    ffn_out = ffn_w_out.astype(BF16)
    x2d = _ffn(attn, fox_w_out[0].astype(BF16), x2d, row(norm_ffn[0]), ffn_in, ffn_out, 0,
               row(final_norm), False, "ffn0")

    half = RET_QK_DIM // 2
    inv = ROPE_BASE ** (-jnp.arange(half, dtype=F32) / half)
    ang = jnp.arange(seq, dtype=F32)[:, None] * inv[None, :]
    w_in = ret_w_in[0]
    q, kt, v, sg = _ret_proj(x2d, row(norm_mix[1]), w_in.astype(BF16),
                             w_in[:, D_MODEL:2 * D_MODEL].T.astype(BF16), jnp.cos(ang), jnp.sin(ang), seq)
    gated = _retention(q, kt, v, sg, batch, seq)
    x2d = _ffn(gated, ret_w_out[0].astype(BF16), x2d, row(norm_ffn[1]), ffn_in, ffn_out, 1,
               row(final_norm), True, "ffn1")
    return x2d.reshape(batch, seq, d)
```

```python
import functools
import math

import jax
import jax.numpy as jnp
import numpy as np
from jax import lax
from jax.experimental import pallas as pl
from jax.experimental.pallas import tpu as pltpu

D_MODEL = 1024
CHUNK = 64
FOX_HEADS = 8
FOX_HEAD_DIM = D_MODEL // FOX_HEADS
RET_HEADS = 4
RET_QK_DIM = D_MODEL // RET_HEADS
RET_V_DIM = 2 * D_MODEL // RET_HEADS
D_FF = -(-8 * D_MODEL // (3 * 256)) * 256
ROPE_BASE = 10000.0
EPS = 1e-6
LOG2E = math.log2(math.e)

F32 = jnp.float32
BF16 = jnp.bfloat16

VMEM_LIMIT_BYTES = 56 * 1024 * 1024
LANES = 128

ROW_TILE = 512
ATTN_Q_TILE = 2 * ROW_TILE
ATTN_K_TILE = ATTN_Q_TILE // 2
RET_BLOCK = 256
FFN_CHUNKS = ((0, 1024), (1024, 1024), (2048, 768))
C_PARTS = 3
ONES_ROWS = 16


def _params(*semantics):
    return pltpu.CompilerParams(dimension_semantics=semantics, vmem_limit_bytes=VMEM_LIMIT_BYTES)


def _rmsnorm(x, g):
    return x * lax.rsqrt(jnp.mean(x * x, axis=-1, keepdims=True) + EPS) * g


def _dot(a, b):
    return jnp.dot(a, b, preferred_element_type=F32)


def _dot_nt(a, b):
    return lax.dot_general(a, b, (((1,), (1,)), ((), ())), preferred_element_type=F32)


def _dot_tn(a, b):
    return lax.dot_general(a, b, (((0,), (0,)), ((), ())), preferred_element_type=F32)


def _cumsum_rows(v):
    n = v.shape[0]
    row = lax.broadcasted_iota(jnp.int32, v.shape, 0)
    shift = 1
    while shift < n:
        rolled = pltpu.roll(v, shift, axis=0)
        v = v + jnp.where(row >= shift, rolled, 0.0)
        shift *= 2
    return v


def _fox_proj_kernel(x_ref, g_ref, wk_ref, wqvt_ref, wf_ref, bf_ref, k_ref, qvt_ref, ca_ref, carry_ref,
                     *, tiles_per_seq):
    i = pl.program_id(0)
    tm = x_ref.shape[0]
    h = _rmsnorm(x_ref[...], g_ref[...]).astype(BF16)
    logits = _dot(h, wf_ref[...]) + bf_ref[...]
    log_f = jnp.minimum(logits, 0.0) - jnp.log1p(jnp.exp(-jnp.abs(logits)))
    carry = jnp.where(i % tiles_per_seq == 0, 0.0, carry_ref[...])
    c = _cumsum_rows(log_f) + carry
    carry_ref[...] = c[tm - 1:tm, :]
    neg_c = c * (-LOG2E)
    hi = neg_c.astype(BF16)
    rest = neg_c - hi.astype(F32)
    mid = rest.astype(BF16)
    lo = (rest - mid.astype(F32)).astype(BF16)
    lane = lax.broadcasted_iota(jnp.int32, neg_c.shape, 1)
    ca_ref[...] = jnp.where(lane < FOX_HEADS, hi, jnp.where(lane < 2 * FOX_HEADS, mid, lo))

    q_scale = FOX_HEAD_DIM ** -0.5 * LOG2E
    tn = 512
    for n0 in range(0, D_MODEL, tn):
        acc = _dot(h, wk_ref[:, n0:n0 + tn]).astype(BF16)
        for d0 in range(0, tn, FOX_HEAD_DIM):
            k_ref[(n0 + d0) // FOX_HEAD_DIM] = acc[:, d0:d0 + FOX_HEAD_DIM]
    for r0 in range(0, 2 * D_MODEL, tn):
        acc = _dot_nt(wqvt_ref[r0:r0 + tn, :], h)
        if r0 < D_MODEL:
            acc = acc * q_scale
        qvt_ref[r0:r0 + tn, :] = acc.astype(BF16)


def _fox_proj(x2d, g, w_k, w_qvt, wf, bf, seq):
    t = x2d.shape[0]
    tm = ROW_TILE
    return pl.pallas_call(
        functools.partial(_fox_proj_kernel, tiles_per_seq=seq // tm),
        grid=(t // tm,),
        in_specs=[
            pl.BlockSpec((tm, D_MODEL), lambda i: (i, 0)),
            pl.BlockSpec((1, D_MODEL), lambda i: (0, 0)),
            pl.BlockSpec((D_MODEL, D_MODEL), lambda i: (0, 0)),
            pl.BlockSpec((2 * D_MODEL, D_MODEL), lambda i: (0, 0)),
            pl.BlockSpec((D_MODEL, LANES), lambda i: (0, 0)),
            pl.BlockSpec((1, LANES), lambda i: (0, 0)),
        ],
        out_specs=[
            pl.BlockSpec((FOX_HEADS, tm, FOX_HEAD_DIM), lambda i: (0, i, 0)),
            pl.BlockSpec((None, 2 * D_MODEL, tm), lambda i: (i, 0, 0)),
            pl.BlockSpec((tm, LANES), lambda i: (i, 0)),
        ],
        out_shape=[
            jax.ShapeDtypeStruct((FOX_HEADS, t, FOX_HEAD_DIM), BF16),
            jax.ShapeDtypeStruct((t // tm, 2 * D_MODEL, tm), BF16),
            jax.ShapeDtypeStruct((t, LANES), BF16),
        ],
        scratch_shapes=[pltpu.VMEM((1, LANES), F32)],
        compiler_params=_params("arbitrary"),
        name="fox_proj",
    )(x2d, g, w_k, w_qvt, wf, bf)


def _fox_attn_kernel(qt_ref, k_ref, ca_ref, vt_ref, o_ref, s_ref, smax_ref, m_ref, acc_ref, done_ref):
    head = pl.program_id(1)
    tq = ATTN_Q_TILE
    tk = ATTN_K_TILE
    tv = vt_ref.shape[2]
    nq = (qt_ref.shape[0] * tv) // tq
    assert tq == 2 * tk and tk == tv

    piece = lax.broadcasted_iota(jnp.int32, (LANES, tq), 0)
    pick = jnp.logical_and(piece % FOX_HEADS == head, piece < C_PARTS * FOX_HEADS)
    ones_rows = jnp.where(pick, 1.0, 0.0).astype(BF16)

    def queries(i):
        return jnp.concatenate([jnp.concatenate([qt_ref[2 * i], qt_ref[2 * i + 1]], axis=1), ones_rows], axis=0)

    def scores(qt_ext, row0, c0=0):
        k_ext = jnp.concatenate([k_ref[pl.ds(row0, tk), :], ca_ref[pl.ds(row0, tk), :]], axis=1)
        return _dot(k_ext, qt_ext[:, c0:])

    def update(s, vt, c0=0, diagonal=False, tile_max=None):
        if diagonal:
            key = lax.broadcasted_iota(jnp.int32, s.shape, 0)
            qry = lax.broadcasted_iota(jnp.int32, s.shape, 1)
            s = jnp.where(qry >= key, s, -jnp.inf)
        if tile_max is None:
            tile_max = jnp.max(s, axis=0, keepdims=True)
        m_old = m_ref[:, c0:]
        m_new = jnp.maximum(m_old, tile_max)
        alpha = jnp.exp2(m_old - m_new)
        p = jnp.exp2(s - m_new).astype(BF16)
        vt_ext = jnp.concatenate([vt, jnp.ones((ONES_ROWS, vt.shape[1]), BF16)], axis=0)
        acc_ref[:, c0:] = alpha * acc_ref[:, c0:] + _dot(vt_ext, p)
        m_ref[:, c0:] = m_new

    def stash(slot, qt_ext, row0):
        s = scores(qt_ext, row0)
        s_ref[slot] = s
        smax_ref[slot] = jnp.max(s, axis=0, keepdims=True)

    def query_tile(i, carry):
        qt_ext = queries(i)
        m_ref[...] = jnp.full_like(m_ref, -jnp.inf)
        acc_ref[...] = jnp.zeros_like(acc_ref)

        def body(j, carry):
            base = pl.multiple_of(j * tq, tq)
            stash(1, qt_ext, base + tk)
            update(s_ref[0], vt_ref[2 * j], tile_max=smax_ref[0])
            stash(0, qt_ext, base + tq)
            update(s_ref[1], vt_ref[2 * j + 1], tile_max=smax_ref[1])
            return carry

        def two_steps(jj, carry):
            body(2 * jj, carry)
            return body(2 * jj + 1, carry)

        lax.fori_loop(0, i // 2, two_steps, 0)

        @pl.when(i % 2 == 1)
        def _():
            body(i - 1, 0)

        base = pl.multiple_of(i * tq, tq)
        s_last = scores(qt_ext, base + tk, tk)
        finish(pl.multiple_of(jnp.maximum(i - 1, 0) * tq, tq))
        update(s_ref[0], vt_ref[2 * i], 0, diagonal=True)
        qt_next = queries(jnp.minimum(i + 1, nq - 1))
        stash(0, qt_next, 0)
        update(s_last, vt_ref[2 * i + 1], tk, diagonal=True)
        done_ref[...] = acc_ref[...]
        return carry

    def finish(row0):
        done = done_ref[...]
        out = done[:FOX_HEAD_DIM] / done[FOX_HEAD_DIM:FOX_HEAD_DIM + 1]
        o_ref[pl.ds(row0, tq), :] = jnp.transpose(out).astype(o_ref.dtype)

    done_ref[...] = jnp.ones_like(done_ref)
    stash(0, queries(0), 0)
    lax.fori_loop(0, nq, query_tile, 0)
    finish((nq - 1) * tq)


def _fox_attn(k, ca, qvt, batch, seq):
    t = k.shape[1]
    tq = ATTN_Q_TILE
    tk = ATTN_K_TILE
    tv = qvt.shape[2]
    hb = D_MODEL // FOX_HEAD_DIM
    return pl.pallas_call(
        _fox_attn_kernel,
        grid=(batch, FOX_HEADS),
        in_specs=[
            pl.BlockSpec((seq // tv, FOX_HEAD_DIM, tv), lambda b, h: (b, h, 0)),
            pl.BlockSpec((None, seq, FOX_HEAD_DIM), lambda b, h: (h, b, 0)),
            pl.BlockSpec((seq, LANES), lambda b, h: (b, 0)),
            pl.BlockSpec((seq // tv, FOX_HEAD_DIM, tv), lambda b, h: (b, hb + h, 0)),
        ],
        out_specs=pl.BlockSpec((None, seq, FOX_HEAD_DIM), lambda b, h: (h, b, 0)),
        out_shape=jax.ShapeDtypeStruct((FOX_HEADS, t, FOX_HEAD_DIM), BF16),
        scratch_shapes=[
            pltpu.VMEM((2, tk, tq), F32),
            pltpu.VMEM((2, 1, tq), F32),
            pltpu.VMEM((1, tq), F32),
            pltpu.VMEM((FOX_HEAD_DIM + ONES_ROWS, tq), F32),
            pltpu.VMEM((FOX_HEAD_DIM + ONES_ROWS, tq), F32),
        ],
        compiler_params=_params("arbitrary", "arbitrary"),
        name="fox_attn",
    )(qvt, k, ca, qvt)


def _ffn_kernel(a_ref, wp_ref, x_ref, g_ref, wi_ref, wo_ref, fg_ref, o_ref, *, final_norm):
    if len(a_ref.shape) == 3:
        mixed = jnp.concatenate([a_ref[hd] for hd in range(a_ref.shape[0])], axis=1)
    else:
        mixed = a_ref[...]
    x = x_ref[...] + _dot(mixed, wp_ref[...])
    h = _rmsnorm(x, g_ref[...]).astype(BF16)
    out = x
    for c0, width in FFN_CHUNKS:
        gate = _dot(h, wi_ref[:, c0:c0 + width])
        up = _dot(h, wi_ref[:, D_FF + c0:D_FF + c0 + width])
        act = (gate * jax.nn.sigmoid(gate) * up).astype(BF16)
        out = out + _dot(act, wo_ref[c0:c0 + width, :])
    if final_norm:
        out = _rmsnorm(out, fg_ref[...])
    o_ref[...] = out


def _ffn(mixed, w_proj, x2d, g, w_in_all, w_out_all, layer, final_g, final_norm, name):
    t = x2d.shape[0]
    tm = ROW_TILE
    if mixed.ndim == 3:
        mixed_spec = pl.BlockSpec((mixed.shape[0], tm, mixed.shape[2]), lambda i: (0, i, 0))
    else:
        mixed_spec = pl.BlockSpec((tm, mixed.shape[1]), lambda i: (i, 0))
    return pl.pallas_call(
        functools.partial(_ffn_kernel, final_norm=final_norm),
        grid=(t // tm,),
        in_specs=[
            mixed_spec,
            pl.BlockSpec(w_proj.shape, lambda i: (0, 0)),
            pl.BlockSpec((tm, D_MODEL), lambda i: (i, 0)),
            pl.BlockSpec((1, D_MODEL), lambda i: (0, 0)),
            pl.BlockSpec((None, D_MODEL, 2 * D_FF), lambda i: (layer, 0, 0), pipeline_mode=pl.Buffered(1)),
            pl.BlockSpec((None, D_FF, D_MODEL), lambda i: (layer, 0, 0), pipeline_mode=pl.Buffered(1)),
            pl.BlockSpec((1, D_MODEL), lambda i: (0, 0)),
        ],
        out_specs=pl.BlockSpec((tm, D_MODEL), lambda i: (i, 0)),
        out_shape=jax.ShapeDtypeStruct((t, D_MODEL), F32),
        compiler_params=_params("arbitrary"),
        name=name,
    )(mixed, w_proj, x2d, g, w_in_all, w_out_all, final_g)


def _ret_proj_kernel(x_ref, g_ref, wq_ref, wkt_ref, wv_ref, wg_ref, cos_ref, sin_ref, cost_ref, sint_ref,
                     q_ref, kt_ref, v_ref, sg_ref):
    h = _rmsnorm(x_ref[...], g_ref[...]).astype(BF16)
    half = RET_QK_DIM // 2
    tn = 512
    cos, sin = cos_ref[...], sin_ref[...]
    for n0 in range(0, D_MODEL, tn):
        acc = _dot(h, wq_ref[:, n0:n0 + tn])
        for d0 in range(0, tn, RET_QK_DIM):
            t1 = acc[:, d0:d0 + half]
            t2 = acc[:, d0 + half:d0 + RET_QK_DIM]
            q_ref[:, n0 + d0:n0 + d0 + half] = (t1 * cos - t2 * sin).astype(BF16)
            q_ref[:, n0 + d0 + half:n0 + d0 + RET_QK_DIM] = (t1 * sin + t2 * cos).astype(BF16)
    cos_t, sin_t = cost_ref[...], sint_ref[...]
    k_scale = RET_QK_DIM ** -0.5
    for r0 in range(0, D_MODEL, RET_QK_DIM):
        acc = _dot_nt(wkt_ref[r0:r0 + RET_QK_DIM, :], h)
        t1 = acc[:half]
        t2 = acc[half:]
        kt_ref[r0:r0 + half, :] = ((t1 * cos_t - t2 * sin_t) * k_scale).astype(BF16)
        kt_ref[r0 + half:r0 + RET_QK_DIM, :] = ((t1 * sin_t + t2 * cos_t) * k_scale).astype(BF16)
    for n0 in range(0, 2 * D_MODEL, tn):
        gate = _dot(h, wg_ref[:, n0:n0 + tn])
        sg_ref[:, n0:n0 + tn] = (gate * jax.nn.sigmoid(gate)).astype(BF16)
    for n0 in range(0, 2 * D_MODEL, tn):
        v_ref[:, n0:n0 + tn] = _dot(h, wv_ref[:, n0:n0 + tn]).astype(BF16)


def _ret_proj(x2d, g, w, w_kt, cos, sin, seq):
    t = x2d.shape[0]
    tm = ROW_TILE
    tiles_per_seq = seq // tm
    half = RET_QK_DIM // 2
    row = lambda i: (i, 0)
    const = lambda i: (0, 0)
    return pl.pallas_call(
        _ret_proj_kernel,
        grid=(t // tm,),
        in_specs=[
            pl.BlockSpec((tm, D_MODEL), row),
            pl.BlockSpec((1, D_MODEL), const),
            pl.BlockSpec((D_MODEL, D_MODEL), const, pipeline_mode=pl.Buffered(1)),
            pl.BlockSpec((D_MODEL, D_MODEL), const),
            pl.BlockSpec((D_MODEL, 2 * D_MODEL), lambda i: (0, 1), pipeline_mode=pl.Buffered(1)),
            pl.BlockSpec((D_MODEL, 2 * D_MODEL), lambda i: (0, 2), pipeline_mode=pl.Buffered(1)),
            pl.BlockSpec((tm, half), lambda i: (i % tiles_per_seq, 0)),
            pl.BlockSpec((tm, half), lambda i: (i % tiles_per_seq, 0)),
            pl.BlockSpec((half, tm), lambda i: (0, i % tiles_per_seq)),
            pl.BlockSpec((half, tm), lambda i: (0, i % tiles_per_seq)),
        ],
        out_specs=[
            pl.BlockSpec((tm, D_MODEL), row),
            pl.BlockSpec((None, D_MODEL, tm), lambda i: (i, 0, 0)),
            pl.BlockSpec((tm, 2 * D_MODEL), row),
            pl.BlockSpec((tm, 2 * D_MODEL), row),
        ],
        out_shape=[
            jax.ShapeDtypeStruct((t, D_MODEL), BF16),
            jax.ShapeDtypeStruct((t // tm, D_MODEL, tm), BF16),
            jax.ShapeDtypeStruct((t, 2 * D_MODEL), BF16),
            jax.ShapeDtypeStruct((t, 2 * D_MODEL), BF16),
        ],
        compiler_params=_params("arbitrary"),
        name="ret_proj",
    )(x2d, g, w, w_kt, w, w, cos, sin, cos.T, sin.T)


def _retention_kernel(q_ref, kt_ref, v_ref, sg_ref, w_ref, qd_ref, kd_ref, o_ref, state_ref, *, block_decay):
    @pl.when(pl.program_id(1) == 0)
    def _():
        state_ref[...] = jnp.zeros_like(state_ref)

    r = RET_BLOCK
    for blk in range(q_ref.shape[0] // r):
        rows = slice(blk * r, (blk + 1) * r)
        for hd in range(RET_HEADS):
            qk_cols = slice(hd * RET_QK_DIM, (hd + 1) * RET_QK_DIM)
            v_cols = slice(hd * RET_V_DIM, (hd + 1) * RET_V_DIM)
            q = q_ref[rows, qk_cols]
            kt = kt_ref[qk_cols, rows]
            v = v_ref[rows, v_cols]
            state = state_ref[hd]
            p = (_dot(q, kt) * w_ref[hd]).astype(BF16)
            q_dec = (q.astype(F32) * qd_ref[hd]).astype(BF16)
            y = _dot(jnp.concatenate([p, q_dec], axis=1), jnp.concatenate([v, state.astype(BF16)], axis=0))
            kt_dec = (kt.astype(F32) * kd_ref[hd]).astype(BF16)
            state_ref[hd] = state * block_decay[hd] + _dot(kt_dec, v)

            mu = jnp.mean(y, axis=-1, keepdims=True)
            yc = y - mu
            var = jnp.mean(yc * yc, axis=-1, keepdims=True)
            normed = (yc * lax.rsqrt(var + EPS)).astype(BF16)
            o_ref[rows, v_cols] = sg_ref[rows, v_cols] * normed


def _retention_tables():
    r = RET_BLOCK
    gamma = 1.0 - 2.0 ** (-5.0 - np.arange(RET_HEADS, dtype=np.float64))
    log_gamma = jnp.log(jnp.asarray(gamma, dtype=F32))
    pos = np.arange(r)
    diff = (pos[:, None] - pos[None, :]).astype(np.float32)
    same = (pos[:, None] // CHUNK) == (pos[None, :] // CHUNK)
    earlier = (pos[None, :] // CHUNK) < (pos[:, None] // CHUNK)
    expo = np.where(same, np.abs(diff), diff)
    w = jnp.where(jnp.asarray(same | earlier)[None],
                  jnp.exp(log_gamma[:, None, None] * jnp.asarray(expo)[None]), 0.0)
    idx = jnp.arange(r, dtype=F32)
    qd = jnp.exp(log_gamma[:, None] * (idx[None, :] + 1.0))
    qd = jnp.broadcast_to(qd[..., None], (RET_HEADS, r, RET_QK_DIM))
    kd = jnp.exp(log_gamma[:, None] * (r - 1.0 - idx[None, :]))[:, None, :]
    block_decay = tuple(float(g) ** r for g in gamma.astype(np.float32).astype(np.float64))
    return w, qd, kd, block_decay


def _retention(q, kt, v, sg, batch, seq):
    t = q.shape[0]
    r = RET_BLOCK
    rows = kt.shape[2]
    nb = seq // rows
    w, qd, kd, block_decay = _retention_tables()
    blk = lambda b, n: (b * nb + n, 0)
    const = lambda b, n: (0, 0, 0)
    return pl.pallas_call(
        functools.partial(_retention_kernel, block_decay=block_decay),
        grid=(batch, nb),
        in_specs=[
            pl.BlockSpec((rows, D_MODEL), blk),
            pl.BlockSpec((None, D_MODEL, rows), lambda b, n: (b * nb + n, 0, 0)),
            pl.BlockSpec((rows, 2 * D_MODEL), blk),
            pl.BlockSpec((rows, 2 * D_MODEL), blk),
            pl.BlockSpec((RET_HEADS, r, r), const),
            pl.BlockSpec((RET_HEADS, r, RET_QK_DIM), const),
            pl.BlockSpec((RET_HEADS, 1, r), const),
        ],
        out_specs=pl.BlockSpec((rows, 2 * D_MODEL), blk),
        out_shape=jax.ShapeDtypeStruct((t, 2 * D_MODEL), BF16),
        scratch_shapes=[pltpu.VMEM((RET_HEADS, RET_QK_DIM, RET_V_DIM), F32)],
        compiler_params=_params("arbitrary", "arbitrary"),
        name="retention",
    )(q, kt, v, sg, w, qd, kd)


def kernel(x, norm_mix, norm_ffn, fox_w_in, fox_b_f, fox_w_out, ret_w_in, ret_w_out,
           ffn_w_in, ffn_w_out, final_norm):
    batch, seq, d = x.shape
    t = batch * seq
    x2d = x.reshape(t, d)
    row = lambda a: a.reshape(1, -1)

    w_in = fox_w_in[0]
    w_k = w_in[:, D_MODEL:2 * D_MODEL].astype(BF16)
    w_qvt = jnp.concatenate([w_in[:, :D_MODEL], w_in[:, 2 * D_MODEL:3 * D_MODEL]], axis=1).T.astype(BF16)
    pad = LANES - C_PARTS * FOX_HEADS
    wf = jnp.pad(jnp.tile(w_in[:, 3 * D_MODEL:], (1, C_PARTS)), ((0, 0), (0, pad))).astype(BF16)
    bf = jnp.pad(jnp.tile(fox_b_f[0], C_PARTS), (0, pad)).reshape(1, LANES)
    k, qvt, ca = _fox_proj(x2d, row(norm_mix[0]), w_k, w_qvt, wf, bf, seq)
    attn = _fox_attn(k, ca, qvt, batch, seq)
    ffn_in = ffn_w_in.astype(BF16)
    ffn_out = ffn_w_out.astype(BF16)
    x2d = _ffn(attn, fox_w_out[0].astype(BF16), x2d, row(norm_ffn[0]), ffn_in, ffn_out, 0,
               row(final_norm), False, "ffn0")

    half = RET_QK_DIM // 2
    inv = ROPE_BASE ** (-jnp.arange(half, dtype=F32) / half)
    ang = jnp.arange(seq, dtype=F32)[:, None] * inv[None, :]
    w_in = ret_w_in[0]
    q, kt, v, sg = _ret_proj(x2d, row(norm_mix[1]), w_in.astype(BF16),
                             w_in[:, D_MODEL:2 * D_MODEL].T.astype(BF16), jnp.cos(ang), jnp.sin(ang), seq)
    gated = _retention(q, kt, v, sg, batch, seq)
    x2d = _ffn(gated, ret_w_out[0].astype(BF16), x2d, row(norm_ffn[1]), ffn_in, ffn_out, 1,
               row(final_norm), True, "ffn1")
    return x2d.reshape(batch, seq, d)
```

```python
import functools
import math

import jax
import jax.numpy as jnp
import numpy as np
from jax import lax
from jax.experimental import pallas as pl
from jax.experimental.pallas import tpu as pltpu

D_MODEL = 1024
CHUNK = 64
FOX_HEADS = 8
FOX_HEAD_DIM = D_MODEL // FOX_HEADS
RET_HEADS = 4
RET_QK_DIM = D_MODEL // RET_HEADS
RET_V_DIM = 2 * D_MODEL // RET_HEADS
D_FF = -(-8 * D_MODEL // (3 * 256)) * 256
ROPE_BASE = 10000.0
EPS = 1e-6
LOG2E = math.log2(math.e)

F32 = jnp.float32
BF16 = jnp.bfloat16

VMEM_LIMIT_BYTES = 56 * 1024 * 1024
LANES = 128

ROW_TILE = 512
ATTN_Q_TILE = 2 * ROW_TILE
ATTN_K_TILE = ATTN_Q_TILE // 2
RET_BLOCK = 256
FFN_CHUNKS = ((0, 1024), (1024, 1024), (2048, 768))
C_PARTS = 3
ONES_ROWS = 16


def _params(*semantics):
    return pltpu.CompilerParams(dimension_semantics=semantics, vmem_limit_bytes=VMEM_LIMIT_BYTES)


def _rmsnorm(x, g):
    return x * lax.rsqrt(jnp.mean(x * x, axis=-1, keepdims=True) + EPS) * g


def _dot(a, b):
    return jnp.dot(a, b, preferred_element_type=F32)


def _dot_nt(a, b):
    return lax.dot_general(a, b, (((1,), (1,)), ((), ())), preferred_element_type=F32)


def _dot_tn(a, b):
    return lax.dot_general(a, b, (((0,), (0,)), ((), ())), preferred_element_type=F32)


def _cumsum_rows(v):
    n = v.shape[0]
    row = lax.broadcasted_iota(jnp.int32, v.shape, 0)
    shift = 1
    while shift < n:
        rolled = pltpu.roll(v, shift, axis=0)
        v = v + jnp.where(row >= shift, rolled, 0.0)
        shift *= 2
    return v


def _fox_proj_kernel(x_ref, g_ref, wk_ref, wqvt_ref, wf_ref, bf_ref, k_ref, qvt_ref, ca_ref, carry_ref,
                     *, tiles_per_seq):
    i = pl.program_id(0)
    tm = x_ref.shape[0]
    h = _rmsnorm(x_ref[...], g_ref[...]).astype(BF16)
    logits = _dot(h, wf_ref[...]) + bf_ref[...]
    log_f = jnp.minimum(logits, 0.0) - jnp.log1p(jnp.exp(-jnp.abs(logits)))
    carry = jnp.where(i % tiles_per_seq == 0, 0.0, carry_ref[...])
    c = _cumsum_rows(log_f) + carry
    carry_ref[...] = c[tm - 1:tm, :]
    neg_c = c * (-LOG2E)
    hi = neg_c.astype(BF16)
    rest = neg_c - hi.astype(F32)
    mid = rest.astype(BF16)
    lo = (rest - mid.astype(F32)).astype(BF16)
    lane = lax.broadcasted_iota(jnp.int32, neg_c.shape, 1)
    ca_ref[...] = jnp.where(lane < FOX_HEADS, hi, jnp.where(lane < 2 * FOX_HEADS, mid, lo))

    q_scale = FOX_HEAD_DIM ** -0.5 * LOG2E
    tn = 512
    for n0 in range(0, D_MODEL, tn):
        acc = _dot(h, wk_ref[:, n0:n0 + tn]).astype(BF16)
        for d0 in range(0, tn, FOX_HEAD_DIM):
            k_ref[(n0 + d0) // FOX_HEAD_DIM] = acc[:, d0:d0 + FOX_HEAD_DIM]
    for r0 in range(0, 2 * D_MODEL, tn):
        acc = jnp.transpose(_dot(h, wqvt_ref[:, r0:r0 + tn]))
        if r0 < D_MODEL:
            acc = acc * q_scale
        qvt_ref[r0:r0 + tn, :] = acc.astype(BF16)


def _fox_proj(x2d, g, w_k, w_qvt, wf, bf, seq):
    t = x2d.shape[0]
    tm = ROW_TILE
    return pl.pallas_call(
        functools.partial(_fox_proj_kernel, tiles_per_seq=seq // tm),
        grid=(t // tm,),
        in_specs=[
            pl.BlockSpec((tm, D_MODEL), lambda i: (i, 0)),
            pl.BlockSpec((1, D_MODEL), lambda i: (0, 0)),
            pl.BlockSpec((D_MODEL, D_MODEL), lambda i: (0, 0)),
            pl.BlockSpec((D_MODEL, 2 * D_MODEL), lambda i: (0, 0)),
            pl.BlockSpec((D_MODEL, LANES), lambda i: (0, 0)),
            pl.BlockSpec((1, LANES), lambda i: (0, 0)),
        ],
        out_specs=[
            pl.BlockSpec((FOX_HEADS, tm, FOX_HEAD_DIM), lambda i: (0, i, 0)),
            pl.BlockSpec((None, 2 * D_MODEL, tm), lambda i: (i, 0, 0)),
            pl.BlockSpec((tm, LANES), lambda i: (i, 0)),
        ],
        out_shape=[
            jax.ShapeDtypeStruct((FOX_HEADS, t, FOX_HEAD_DIM), BF16),
            jax.ShapeDtypeStruct((t // tm, 2 * D_MODEL, tm), BF16),
            jax.ShapeDtypeStruct((t, LANES), BF16),
        ],
        scratch_shapes=[pltpu.VMEM((1, LANES), F32)],
        compiler_params=_params("arbitrary"),
        name="fox_proj",
    )(x2d, g, w_k, w_qvt, wf, bf)


def _fox_attn_kernel(qt_ref, k_ref, ca_ref, vt_ref, o_ref, s_ref, smax_ref, m_ref, acc_ref, done_ref):
    head = pl.program_id(1)
    tq = ATTN_Q_TILE
    tk = ATTN_K_TILE
    tv = vt_ref.shape[2]
    nq = (qt_ref.shape[0] * tv) // tq
    assert tq == 2 * tk and tk == tv

    piece = lax.broadcasted_iota(jnp.int32, (LANES, tq), 0)
    pick = jnp.logical_and(piece % FOX_HEADS == head, piece < C_PARTS * FOX_HEADS)
    ones_rows = jnp.where(pick, 1.0, 0.0).astype(BF16)

    def queries(i):
        return jnp.concatenate([jnp.concatenate([qt_ref[2 * i], qt_ref[2 * i + 1]], axis=1), ones_rows], axis=0)

    def scores(qt_ext, row0, c0=0):
        k_ext = jnp.concatenate([k_ref[pl.ds(row0, tk), :], ca_ref[pl.ds(row0, tk), :]], axis=1)
        return _dot(k_ext, qt_ext[:, c0:])

    def update(s, vt, c0=0, diagonal=False, tile_max=None):
        if diagonal:
            key = lax.broadcasted_iota(jnp.int32, s.shape, 0)
            qry = lax.broadcasted_iota(jnp.int32, s.shape, 1)
            s = jnp.where(qry >= key, s, -jnp.inf)
        if tile_max is None:
            tile_max = jnp.max(s, axis=0, keepdims=True)
        m_old = m_ref[:, c0:]
        m_new = jnp.maximum(m_old, tile_max)
        alpha = jnp.exp2(m_old - m_new)
        p = jnp.exp2(s - m_new).astype(BF16)
        vt_ext = jnp.concatenate([vt, jnp.ones((ONES_ROWS, vt.shape[1]), BF16)], axis=0)
        acc_ref[:, c0:] = alpha * acc_ref[:, c0:] + _dot(vt_ext, p)
        m_ref[:, c0:] = m_new

    def stash(slot, qt_ext, row0):
        s = scores(qt_ext, row0)
        s_ref[slot] = s
        smax_ref[slot] = jnp.max(s, axis=0, keepdims=True)

    def query_tile(i, carry):
        qt_ext = queries(i)
        m_ref[...] = jnp.full_like(m_ref, -jnp.inf)
        acc_ref[...] = jnp.zeros_like(acc_ref)

        def body(j, carry):
            base = pl.multiple_of(j * tq, tq)
            stash(1, qt_ext, base + tk)
            update(s_ref[0], vt_ref[2 * j], tile_max=smax_ref[0])
            stash(0, qt_ext, base + tq)
            update(s_ref[1], vt_ref[2 * j + 1], tile_max=smax_ref[1])
            return carry

        def four_steps(jj, carry):
            for step in range(4):
                body(4 * jj + step, carry)
            return carry

        lax.fori_loop(0, i // 4, four_steps, 0)

        @pl.when(i % 4 >= 2)
        def _():
            first = (i // 4) * 4
            body(first, 0)
            body(first + 1, 0)

        @pl.when(i % 2 == 1)
        def _():
            body(i - 1, 0)

        base = pl.multiple_of(i * tq, tq)
        s_last = scores(qt_ext, base + tk, tk)
        finish(pl.multiple_of(jnp.maximum(i - 1, 0) * tq, tq))
        update(s_ref[0], vt_ref[2 * i], 0, diagonal=True)
        qt_next = queries(jnp.minimum(i + 1, nq - 1))
        stash(0, qt_next, 0)
        update(s_last, vt_ref[2 * i + 1], tk, diagonal=True)
        done_ref[...] = acc_ref[...]
        return carry

    def finish(row0):
        done = done_ref[...]
        out = done[:FOX_HEAD_DIM] / done[FOX_HEAD_DIM:FOX_HEAD_DIM + 1]
        o_ref[pl.ds(row0, tq), :] = jnp.transpose(out).astype(o_ref.dtype)

    done_ref[...] = jnp.ones_like(done_ref)
    stash(0, queries(0), 0)
    lax.fori_loop(0, nq, query_tile, 0)
    finish((nq - 1) * tq)


def _fox_attn(k, ca, qvt, batch, seq):
    t = k.shape[1]
    tq = ATTN_Q_TILE
    tk = ATTN_K_TILE
    tv = qvt.shape[2]
    hb = D_MODEL // FOX_HEAD_DIM
    return pl.pallas_call(
        _fox_attn_kernel,
        grid=(batch, FOX_HEADS),
        in_specs=[
            pl.BlockSpec((seq // tv, FOX_HEAD_DIM, tv), lambda b, h: (b, h, 0)),
            pl.BlockSpec((None, seq, FOX_HEAD_DIM), lambda b, h: (h, b, 0)),
            pl.BlockSpec((seq, LANES), lambda b, h: (b, 0)),
            pl.BlockSpec((seq // tv, FOX_HEAD_DIM, tv), lambda b, h: (b, hb + h, 0)),
        ],
        out_specs=pl.BlockSpec((None, seq, FOX_HEAD_DIM), lambda b, h: (h, b, 0)),
        out_shape=jax.ShapeDtypeStruct((FOX_HEADS, t, FOX_HEAD_DIM), BF16),
        scratch_shapes=[
            pltpu.VMEM((2, tk, tq), F32),
            pltpu.VMEM((2, 1, tq), F32),
            pltpu.VMEM((1, tq), F32),
            pltpu.VMEM((FOX_HEAD_DIM + ONES_ROWS, tq), F32),
            pltpu.VMEM((FOX_HEAD_DIM + ONES_ROWS, tq), F32),
        ],
        compiler_params=_params("arbitrary", "arbitrary"),
        name="fox_attn",
    )(qvt, k, ca, qvt)


def _ffn_kernel(a_ref, wp_ref, x_ref, g_ref, wi_ref, wo_ref, fg_ref, o_ref, *, final_norm):
    if len(a_ref.shape) == 3:
        mixed = jnp.concatenate([a_ref[hd] for hd in range(a_ref.shape[0])], axis=1)
    else:
        mixed = a_ref[...]
    x = x_ref[...] + _dot(mixed, wp_ref[...])
    h = _rmsnorm(x, g_ref[...]).astype(BF16)
    out = x
    for c0, width in FFN_CHUNKS:
        gate = _dot(h, wi_ref[:, c0:c0 + width])
        up = _dot(h, wi_ref[:, D_FF + c0:D_FF + c0 + width])
        act = (gate * jax.nn.sigmoid(gate) * up).astype(BF16)
        out = out + _dot(act, wo_ref[c0:c0 + width, :])
    if final_norm:
        out = _rmsnorm(out, fg_ref[...])
    o_ref[...] = out


def _ffn(mixed, w_proj, x2d, g, w_in_all, w_out_all, layer, final_g, final_norm, name):
    t = x2d.shape[0]
    tm = ROW_TILE
    if mixed.ndim == 3:
        mixed_spec = pl.BlockSpec((mixed.shape[0], tm, mixed.shape[2]), lambda i: (0, i, 0))
    else:
        mixed_spec = pl.BlockSpec((tm, mixed.shape[1]), lambda i: (i, 0))
    return pl.pallas_call(
        functools.partial(_ffn_kernel, final_norm=final_norm),
        grid=(t // tm,),
        in_specs=[
            mixed_spec,
            pl.BlockSpec(w_proj.shape, lambda i: (0, 0)),
            pl.BlockSpec((tm, D_MODEL), lambda i: (i, 0)),
            pl.BlockSpec((1, D_MODEL), lambda i: (0, 0)),
            pl.BlockSpec((None, D_MODEL, 2 * D_FF), lambda i: (layer, 0, 0), pipeline_mode=pl.Buffered(1)),
            pl.BlockSpec((None, D_FF, D_MODEL), lambda i: (layer, 0, 0), pipeline_mode=pl.Buffered(1)),
            pl.BlockSpec((1, D_MODEL), lambda i: (0, 0)),
        ],
        out_specs=pl.BlockSpec((tm, D_MODEL), lambda i: (i, 0)),
        out_shape=jax.ShapeDtypeStruct((t, D_MODEL), F32),
        compiler_params=_params("arbitrary"),
        name=name,
    )(mixed, w_proj, x2d, g, w_in_all, w_out_all, final_g)


def _ret_proj_kernel(x_ref, g_ref, wq_ref, wk_ref, wv_ref, wg_ref, cos_ref, sin_ref, cost_ref, sint_ref,
                     q_ref, kt_ref, v_ref, sg_ref):
    h = _rmsnorm(x_ref[...], g_ref[...]).astype(BF16)
    half = RET_QK_DIM // 2
    tn = 512
    cos, sin = cos_ref[...], sin_ref[...]
    for n0 in range(0, D_MODEL, tn):
        acc = _dot(h, wq_ref[:, n0:n0 + tn])
        for d0 in range(0, tn, RET_QK_DIM):
            t1 = acc[:, d0:d0 + half]
            t2 = acc[:, d0 + half:d0 + RET_QK_DIM]
            q_ref[:, n0 + d0:n0 + d0 + half] = (t1 * cos - t2 * sin).astype(BF16)
            q_ref[:, n0 + d0 + half:n0 + d0 + RET_QK_DIM] = (t1 * sin + t2 * cos).astype(BF16)
    cos_t, sin_t = cost_ref[...], sint_ref[...]
    k_scale = RET_QK_DIM ** -0.5
    for r0 in range(0, D_MODEL, RET_QK_DIM):
        acc = jnp.transpose(_dot(h, wk_ref[:, r0:r0 + RET_QK_DIM]))
        t1 = acc[:half]
        t2 = acc[half:]
        kt_ref[r0:r0 + half, :] = ((t1 * cos_t - t2 * sin_t) * k_scale).astype(BF16)
        kt_ref[r0 + half:r0 + RET_QK_DIM, :] = ((t1 * sin_t + t2 * cos_t) * k_scale).astype(BF16)
    for n0 in range(0, 2 * D_MODEL, tn):
        gate = _dot(h, wg_ref[:, n0:n0 + tn])
        sg_ref[:, n0:n0 + tn] = (gate * jax.nn.sigmoid(gate)).astype(BF16)
    for n0 in range(0, 2 * D_MODEL, tn):
        v_ref[:, n0:n0 + tn] = _dot(h, wv_ref[:, n0:n0 + tn]).astype(BF16)


def _ret_proj(x2d, g, w, cos, sin, seq):
    t = x2d.shape[0]
    tm = ROW_TILE
    tiles_per_seq = seq // tm
    half = RET_QK_DIM // 2
    row = lambda i: (i, 0)
    const = lambda i: (0, 0)
    return pl.pallas_call(
        _ret_proj_kernel,
        grid=(t // tm,),
        in_specs=[
            pl.BlockSpec((tm, D_MODEL), row),
            pl.BlockSpec((1, D_MODEL), const),
            pl.BlockSpec((D_MODEL, D_MODEL), const, pipeline_mode=pl.Buffered(1)),
            pl.BlockSpec((D_MODEL, D_MODEL), lambda i: (0, 1), pipeline_mode=pl.Buffered(1)),
            pl.BlockSpec((D_MODEL, 2 * D_MODEL), lambda i: (0, 1), pipeline_mode=pl.Buffered(1)),
            pl.BlockSpec((D_MODEL, 2 * D_MODEL), lambda i: (0, 2), pipeline_mode=pl.Buffered(1)),
            pl.BlockSpec((tm, half), lambda i: (i % tiles_per_seq, 0)),
            pl.BlockSpec((tm, half), lambda i: (i % tiles_per_seq, 0)),
            pl.BlockSpec((half, tm), lambda i: (0, i % tiles_per_seq)),
            pl.BlockSpec((half, tm), lambda i: (0, i % tiles_per_seq)),
        ],
        out_specs=[
            pl.BlockSpec((tm, D_MODEL), row),
            pl.BlockSpec((None, D_MODEL, tm), lambda i: (i, 0, 0)),
            pl.BlockSpec((tm, 2 * D_MODEL), row),
            pl.BlockSpec((tm, 2 * D_MODEL), row),
        ],
        out_shape=[
            jax.ShapeDtypeStruct((t, D_MODEL), BF16),
            jax.ShapeDtypeStruct((t // tm, D_MODEL, tm), BF16),
            jax.ShapeDtypeStruct((t, 2 * D_MODEL), BF16),
            jax.ShapeDtypeStruct((t, 2 * D_MODEL), BF16),
        ],
        compiler_params=_params("arbitrary"),
        name="ret_proj",
    )(x2d, g, w, w, w, w, cos, sin, cos.T, sin.T)


def _retention_kernel(q_ref, kt_ref, v_ref, sg_ref, w_ref, qd_ref, kd_ref, o_ref, state_ref, *, block_decay):
    @pl.when(pl.program_id(1) == 0)
    def _():
        state_ref[...] = jnp.zeros_like(state_ref)

    r = RET_BLOCK
    for blk in range(q_ref.shape[0] // r):
        rows = slice(blk * r, (blk + 1) * r)
        for hd in range(RET_HEADS):
            qk_cols = slice(hd * RET_QK_DIM, (hd + 1) * RET_QK_DIM)
            v_cols = slice(hd * RET_V_DIM, (hd + 1) * RET_V_DIM)
            q = q_ref[rows, qk_cols]
            kt = kt_ref[qk_cols, rows]
            v = v_ref[rows, v_cols]
            state = state_ref[hd]
            p = (_dot(q, kt) * w_ref[hd]).astype(BF16)
            q_dec = (q.astype(F32) * qd_ref[hd]).astype(BF16)
            y = _dot(jnp.concatenate([p, q_dec], axis=1), jnp.concatenate([v, state.astype(BF16)], axis=0))
            kt_dec = (kt.astype(F32) * kd_ref[hd]).astype(BF16)
            state_ref[hd] = state * block_decay[hd] + _dot(kt_dec, v)

            mu = jnp.mean(y, axis=-1, keepdims=True)
            yc = y - mu
            var = jnp.mean(yc * yc, axis=-1, keepdims=True)
            normed = (yc * lax.rsqrt(var + EPS)).astype(BF16)
            o_ref[rows, v_cols] = sg_ref[rows, v_cols] * normed


def _retention_tables():
    r = RET_BLOCK
    gamma = 1.0 - 2.0 ** (-5.0 - np.arange(RET_HEADS, dtype=np.float64))
    log_gamma = jnp.log(jnp.asarray(gamma, dtype=F32))
    pos = np.arange(r)
    diff = (pos[:, None] - pos[None, :]).astype(np.float32)
    same = (pos[:, None] // CHUNK) == (pos[None, :] // CHUNK)
    earlier = (pos[None, :] // CHUNK) < (pos[:, None] // CHUNK)
    expo = np.where(same, np.abs(diff), diff)
    w = jnp.where(jnp.asarray(same | earlier)[None],
                  jnp.exp(log_gamma[:, None, None] * jnp.asarray(expo)[None]), 0.0)
    idx = jnp.arange(r, dtype=F32)
    qd = jnp.exp(log_gamma[:, None] * (idx[None, :] + 1.0))
    qd = jnp.broadcast_to(qd[..., None], (RET_HEADS, r, RET_QK_DIM))
    kd = jnp.exp(log_gamma[:, None] * (r - 1.0 - idx[None, :]))[:, None, :]
    block_decay = tuple(float(g) ** r for g in gamma.astype(np.float32).astype(np.float64))
    return w, qd, kd, block_decay


def _retention(q, kt, v, sg, batch, seq):
    t = q.shape[0]
    r = RET_BLOCK
    rows = kt.shape[2]
    nb = seq // rows
    w, qd, kd, block_decay = _retention_tables()
    blk = lambda b, n: (b * nb + n, 0)
    const = lambda b, n: (0, 0, 0)
    return pl.pallas_call(
        functools.partial(_retention_kernel, block_decay=block_decay),
        grid=(batch, nb),
        in_specs=[
            pl.BlockSpec((rows, D_MODEL), blk),
            pl.BlockSpec((None, D_MODEL, rows), lambda b, n: (b * nb + n, 0, 0)),
            pl.BlockSpec((rows, 2 * D_MODEL), blk),
            pl.BlockSpec((rows, 2 * D_MODEL), blk),
            pl.BlockSpec((RET_HEADS, r, r), const),
            pl.BlockSpec((RET_HEADS, r, RET_QK_DIM), const),
            pl.BlockSpec((RET_HEADS, 1, r), const),
        ],
        out_specs=pl.BlockSpec((rows, 2 * D_MODEL), blk),
        out_shape=jax.ShapeDtypeStruct((t, 2 * D_MODEL), BF16),
        scratch_shapes=[pltpu.VMEM((RET_HEADS, RET_QK_DIM, RET_V_DIM), F32)],
        compiler_params=_params("arbitrary", "arbitrary"),
        name="retention",
    )(q, kt, v, sg, w, qd, kd)


def kernel(x, norm_mix, norm_ffn, fox_w_in, fox_b_f, fox_w_out, ret_w_in, ret_w_out,
           ffn_w_in, ffn_w_out, final_norm):
    batch, seq, d = x.shape
    t = batch * seq
    x2d = x.reshape(t, d)
    row = lambda a: a.reshape(1, -1)

    w_in = fox_w_in[0]
    w_k = w_in[:, D_MODEL:2 * D_MODEL].astype(BF16)
    w_qvt = jnp.concatenate([w_in[:, :D_MODEL], w_in[:, 2 * D_MODEL:3 * D_MODEL]], axis=1).astype(BF16)
    pad = LANES - C_PARTS * FOX_HEADS
    wf = jnp.pad(jnp.tile(w_in[:, 3 * D_MODEL:], (1, C_PARTS)), ((0, 0), (0, pad))).astype(BF16)
    bf = jnp.pad(jnp.tile(fox_b_f[0], C_PARTS), (0, pad)).reshape(1, LANES)
    k, qvt, ca = _fox_proj(x2d, row(norm_mix[0]), w_k, w_qvt, wf, bf, seq)
    attn = _fox_attn(k, ca, qvt, batch, seq)
    ffn_in = ffn_w_in.astype(BF16)
    ffn_out = ffn_w_out.astype(BF16)
    x2d = _ffn(attn, fox_w_out[0].astype(BF16), x2d, row(norm_ffn[0]), ffn_in, ffn_out, 0,
               row(final_norm), False, "ffn0")

    half = RET_QK_DIM // 2
    inv = ROPE_BASE ** (-jnp.arange(half, dtype=F32) / half)
    ang = jnp.arange(seq, dtype=F32)[:, None] * inv[None, :]
    w_in = ret_w_in[0]
    q, kt, v, sg = _ret_proj(x2d, row(norm_mix[1]), w_in.astype(BF16), jnp.cos(ang), jnp.sin(ang), seq)
    gated = _retention(q, kt, v, sg, batch, seq)
    x2d = _ffn(gated, ret_w_out[0].astype(BF16), x2d, row(norm_ffn[1]), ffn_in, ffn_out, 1,
               row(final_norm), True, "ffn1")
    return x2d.reshape(batch, seq, d)
```

```python
import functools
import math

import jax
import jax.numpy as jnp
import numpy as np
from jax import lax
from jax.experimental import pallas as pl
from jax.experimental.pallas import tpu as pltpu

D_MODEL = 1024
CHUNK = 64
FOX_HEADS = 8
FOX_HEAD_DIM = D_MODEL // FOX_HEADS
RET_HEADS = 4
RET_QK_DIM = D_MODEL // RET_HEADS
RET_V_DIM = 2 * D_MODEL // RET_HEADS
D_FF = -(-8 * D_MODEL // (3 * 256)) * 256
ROPE_BASE = 10000.0
EPS = 1e-6
LOG2E = math.log2(math.e)

F32 = jnp.float32
BF16 = jnp.bfloat16

VMEM_LIMIT_BYTES = 56 * 1024 * 1024
LANES = 128

ROW_TILE = 512
ATTN_Q_TILE = 2 * ROW_TILE
ATTN_K_TILE = ATTN_Q_TILE // 2
RET_BLOCK = 256
FFN_CHUNKS = ((0, 1024), (1024, 1024), (2048, 768))
C_PARTS = 3
ONES_ROWS = 16


def _params(*semantics):
    return pltpu.CompilerParams(dimension_semantics=semantics, vmem_limit_bytes=VMEM_LIMIT_BYTES)


def _rmsnorm(x, g):
    return x * lax.rsqrt(jnp.mean(x * x, axis=-1, keepdims=True) + EPS) * g


def _dot(a, b):
    return jnp.dot(a, b, preferred_element_type=F32)


def _dot_nt(a, b):
    return lax.dot_general(a, b, (((1,), (1,)), ((), ())), preferred_element_type=F32)


def _dot_tn(a, b):
    return lax.dot_general(a, b, (((0,), (0,)), ((), ())), preferred_element_type=F32)


def _cumsum_rows(v):
    n = v.shape[0]
    row = lax.broadcasted_iota(jnp.int32, v.shape, 0)
    shift = 1
    while shift < n:
        rolled = pltpu.roll(v, shift, axis=0)
        v = v + jnp.where(row >= shift, rolled, 0.0)
        shift *= 2
    return v


def _fox_proj_kernel(x_ref, g_ref, wq_ref, wk_ref, wv_ref, wf_ref, bf_ref, k_ref, qvt_ref, ca_ref, carry_ref,
                     *, tiles_per_seq):
    i = pl.program_id(0)
    tm = x_ref.shape[0]
    h = _rmsnorm(x_ref[...], g_ref[...]).astype(BF16)
    logits = _dot(h, wf_ref[...]) + bf_ref[...]
    log_f = jnp.minimum(logits, 0.0) - jnp.log1p(jnp.exp(-jnp.abs(logits)))
    carry = jnp.where(i % tiles_per_seq == 0, 0.0, carry_ref[...])
    c = _cumsum_rows(log_f) + carry
    carry_ref[...] = c[tm - 1:tm, :]
    neg_c = c * (-LOG2E)
    hi = neg_c.astype(BF16)
    rest = neg_c - hi.astype(F32)
    mid = rest.astype(BF16)
    lo = (rest - mid.astype(F32)).astype(BF16)
    lane = lax.broadcasted_iota(jnp.int32, neg_c.shape, 1)
    ca_ref[...] = jnp.where(lane < FOX_HEADS, hi, jnp.where(lane < 2 * FOX_HEADS, mid, lo))

    q_scale = FOX_HEAD_DIM ** -0.5 * LOG2E
    tn = 512
    for n0 in range(0, D_MODEL, tn):
        acc = _dot(h, wk_ref[:, n0:n0 + tn]).astype(BF16)
        for d0 in range(0, tn, FOX_HEAD_DIM):
            k_ref[(n0 + d0) // FOX_HEAD_DIM] = acc[:, d0:d0 + FOX_HEAD_DIM]
    for n0 in range(0, D_MODEL, tn):
        acc = jnp.transpose(_dot(h, wq_ref[:, n0:n0 + tn])) * q_scale
        qvt_ref[n0:n0 + tn, :] = acc.astype(BF16)
    for n0 in range(0, D_MODEL, tn):
        acc = jnp.transpose(_dot(h, wv_ref[:, n0:n0 + tn]))
        qvt_ref[D_MODEL + n0:D_MODEL + n0 + tn, :] = acc.astype(BF16)


def _fox_proj(x2d, g, w_qkv, wf, bf, seq):
    t = x2d.shape[0]
    tm = ROW_TILE
    window = lambda col: pl.BlockSpec((D_MODEL, D_MODEL), lambda i: (0, col), pipeline_mode=pl.Buffered(1))
    return pl.pallas_call(
        functools.partial(_fox_proj_kernel, tiles_per_seq=seq // tm),
        grid=(t // tm,),
        in_specs=[
            pl.BlockSpec((tm, D_MODEL), lambda i: (i, 0)),
            pl.BlockSpec((1, D_MODEL), lambda i: (0, 0)),
            window(0),
            window(1),
            window(2),
            pl.BlockSpec((D_MODEL, LANES), lambda i: (0, 0)),
            pl.BlockSpec((1, LANES), lambda i: (0, 0)),
        ],
        out_specs=[
            pl.BlockSpec((FOX_HEADS, tm, FOX_HEAD_DIM), lambda i: (0, i, 0)),
            pl.BlockSpec((None, 2 * D_MODEL, tm), lambda i: (i, 0, 0)),
            pl.BlockSpec((tm, LANES), lambda i: (i, 0)),
        ],
        out_shape=[
            jax.ShapeDtypeStruct((FOX_HEADS, t, FOX_HEAD_DIM), BF16),
            jax.ShapeDtypeStruct((t // tm, 2 * D_MODEL, tm), BF16),
            jax.ShapeDtypeStruct((t, LANES), BF16),
        ],
        scratch_shapes=[pltpu.VMEM((1, LANES), F32)],
        compiler_params=_params("arbitrary"),
        name="fox_proj",
    )(x2d, g, w_qkv, w_qkv, w_qkv, wf, bf)


def _fox_attn_kernel(qt_ref, k_ref, ca_ref, vt_ref, o_ref, s_ref, smax_ref, m_ref, acc_ref, done_ref):
    head = pl.program_id(1)
    tq = ATTN_Q_TILE
    tk = ATTN_K_TILE
    tv = vt_ref.shape[2]
    nq = (qt_ref.shape[0] * tv) // tq
    assert tq == 2 * tk and tk == tv

    piece = lax.broadcasted_iota(jnp.int32, (LANES, tq), 0)
    pick = jnp.logical_and(piece % FOX_HEADS == head, piece < C_PARTS * FOX_HEADS)
    ones_rows = jnp.where(pick, 1.0, 0.0).astype(BF16)

    def queries(i):
        return jnp.concatenate([jnp.concatenate([qt_ref[2 * i], qt_ref[2 * i + 1]], axis=1), ones_rows], axis=0)

    def scores(qt_ext, row0, c0=0):
        k_ext = jnp.concatenate([k_ref[pl.ds(row0, tk), :], ca_ref[pl.ds(row0, tk), :]], axis=1)
        return _dot(k_ext, qt_ext[:, c0:])

    def update(s, vt, c0=0, diagonal=False, tile_max=None):
        if diagonal:
            key = lax.broadcasted_iota(jnp.int32, s.shape, 0)
            qry = lax.broadcasted_iota(jnp.int32, s.shape, 1)
            s = jnp.where(qry >= key, s, -jnp.inf)
        if tile_max is None:
            tile_max = jnp.max(s, axis=0, keepdims=True)
        m_old = m_ref[:, c0:]
        m_new = jnp.maximum(m_old, tile_max)
        alpha = jnp.exp2(m_old - m_new)
        p = jnp.exp2(s - m_new).astype(BF16)
        vt_ext = jnp.concatenate([vt, jnp.ones((ONES_ROWS, vt.shape[1]), BF16)], axis=0)
        acc_ref[:, c0:] = alpha * acc_ref[:, c0:] + _dot(vt_ext, p)
        m_ref[:, c0:] = m_new

    def stash(slot, qt_ext, row0):
        s = scores(qt_ext, row0)
        s_ref[slot] = s
        smax_ref[slot] = jnp.max(s, axis=0, keepdims=True)

    def query_tile(i, carry):
        qt_ext = queries(i)
        m_ref[...] = jnp.full_like(m_ref, -jnp.inf)
        acc_ref[...] = jnp.zeros_like(acc_ref)

        def body(j, carry):
            base = pl.multiple_of(j * tq, tq)
            stash(1, qt_ext, base + tk)
            update(s_ref[0], vt_ref[2 * j], tile_max=smax_ref[0])
            stash(0, qt_ext, base + tq)
            update(s_ref[1], vt_ref[2 * j + 1], tile_max=smax_ref[1])
            return carry

        def two_steps(jj, carry):
            body(2 * jj, carry)
            return body(2 * jj + 1, carry)

        lax.fori_loop(0, i // 2, two_steps, 0)

        @pl.when(i % 2 == 1)
        def _():
            body(i - 1, 0)

        base = pl.multiple_of(i * tq, tq)
        s_last = scores(qt_ext, base + tk, tk)
        finish(pl.multiple_of(jnp.maximum(i - 1, 0) * tq, tq))
        update(s_ref[0], vt_ref[2 * i], 0, diagonal=True)
        qt_next = queries(jnp.minimum(i + 1, nq - 1))
        stash(0, qt_next, 0)
        update(s_last, vt_ref[2 * i + 1], tk, diagonal=True)
        done_ref[...] = acc_ref[...]
        return carry

    def finish(row0):
        done = done_ref[...]
        out = done[:FOX_HEAD_DIM] / done[FOX_HEAD_DIM:FOX_HEAD_DIM + 1]
        o_ref[pl.ds(row0, tq), :] = jnp.transpose(out).astype(o_ref.dtype)

    done_ref[...] = jnp.ones_like(done_ref)
    stash(0, queries(0), 0)
    lax.fori_loop(0, nq, query_tile, 0)
    finish((nq - 1) * tq)


def _fox_attn(k, ca, qvt, batch, seq):
    t = k.shape[1]
    tq = ATTN_Q_TILE
    tk = ATTN_K_TILE
    tv = qvt.shape[2]
    hb = D_MODEL // FOX_HEAD_DIM
    return pl.pallas_call(
        _fox_attn_kernel,
        grid=(batch, FOX_HEADS),
        in_specs=[
            pl.BlockSpec((seq // tv, FOX_HEAD_DIM, tv), lambda b, h: (b, h, 0)),
            pl.BlockSpec((None, seq, FOX_HEAD_DIM), lambda b, h: (h, b, 0)),
            pl.BlockSpec((seq, LANES), lambda b, h: (b, 0)),
            pl.BlockSpec((seq // tv, FOX_HEAD_DIM, tv), lambda b, h: (b, hb + h, 0)),
        ],
        out_specs=pl.BlockSpec((None, seq, FOX_HEAD_DIM), lambda b, h: (h, b, 0)),
        out_shape=jax.ShapeDtypeStruct((FOX_HEADS, t, FOX_HEAD_DIM), BF16),
        scratch_shapes=[
            pltpu.VMEM((2, tk, tq), F32),
            pltpu.VMEM((2, 1, tq), F32),
            pltpu.VMEM((1, tq), F32),
            pltpu.VMEM((FOX_HEAD_DIM + ONES_ROWS, tq), F32),
            pltpu.VMEM((FOX_HEAD_DIM + ONES_ROWS, tq), F32),
        ],
        compiler_params=_params("arbitrary", "arbitrary"),
        name="fox_attn",
    )(qvt, k, ca, qvt)


def _ffn_kernel(a_ref, wp_ref, x_ref, g_ref, wi_ref, wo_ref, fg_ref, o_ref, *, final_norm):
    if len(a_ref.shape) == 3:
        mixed = jnp.concatenate([a_ref[hd] for hd in range(a_ref.shape[0])], axis=1)
    else:
        mixed = a_ref[...]
    x = x_ref[...] + _dot(mixed, wp_ref[...])
    h = _rmsnorm(x, g_ref[...]).astype(BF16)
    out = x
    for c0, width in FFN_CHUNKS:
        gate = _dot(h, wi_ref[:, c0:c0 + width])
        up = _dot(h, wi_ref[:, D_FF + c0:D_FF + c0 + width])
        act = (gate * jax.nn.sigmoid(gate) * up).astype(BF16)
        out = out + _dot(act, wo_ref[c0:c0 + width, :])
    if final_norm:
        out = _rmsnorm(out, fg_ref[...])
    o_ref[...] = out


def _ffn(mixed, w_proj, x2d, g, w_in_all, w_out_all, layer, final_g, final_norm, name):
    t = x2d.shape[0]
    tm = ROW_TILE
    if mixed.ndim == 3:
        mixed_spec = pl.BlockSpec((mixed.shape[0], tm, mixed.shape[2]), lambda i: (0, i, 0))
    else:
        mixed_spec = pl.BlockSpec((tm, mixed.shape[1]), lambda i: (i, 0))
    return pl.pallas_call(
        functools.partial(_ffn_kernel, final_norm=final_norm),
        grid=(t // tm,),
        in_specs=[
            mixed_spec,
            pl.BlockSpec(w_proj.shape, lambda i: (0, 0)),
            pl.BlockSpec((tm, D_MODEL), lambda i: (i, 0)),
            pl.BlockSpec((1, D_MODEL), lambda i: (0, 0)),
            pl.BlockSpec((None, D_MODEL, 2 * D_FF), lambda i: (layer, 0, 0), pipeline_mode=pl.Buffered(1)),
            pl.BlockSpec((None, D_FF, D_MODEL), lambda i: (layer, 0, 0), pipeline_mode=pl.Buffered(1)),
            pl.BlockSpec((1, D_MODEL), lambda i: (0, 0)),
        ],
        out_specs=pl.BlockSpec((tm, D_MODEL), lambda i: (i, 0)),
        out_shape=jax.ShapeDtypeStruct((t, D_MODEL), F32),
        compiler_params=_params("arbitrary"),
        name=name,
    )(mixed, w_proj, x2d, g, w_in_all, w_out_all, final_g)


def _ret_project_head(hd, h, wq_ref, wk_ref, wv_ref, wg_ref, cos_ref, sin_ref, cost_ref, sint_ref,
                      q_ref, kt_ref, v_ref, sg_ref):
    half = RET_QK_DIM // 2
    c0 = hd * RET_QK_DIM
    cos, sin = cos_ref[...], sin_ref[...]
    acc = _dot(h, wq_ref[:, c0:c0 + RET_QK_DIM])
    t1, t2 = acc[:, :half], acc[:, half:]
    q_ref[:, c0:c0 + half] = (t1 * cos - t2 * sin).astype(BF16)
    q_ref[:, c0 + half:c0 + RET_QK_DIM] = (t1 * sin + t2 * cos).astype(BF16)
    cos_t, sin_t = cost_ref[...], sint_ref[...]
    k_scale = RET_QK_DIM ** -0.5
    acc = jnp.transpose(_dot(h, wk_ref[:, c0:c0 + RET_QK_DIM]))
    t1, t2 = acc[:half], acc[half:]
    kt_ref[c0:c0 + half, :] = ((t1 * cos_t - t2 * sin_t) * k_scale).astype(BF16)
    kt_ref[c0 + half:c0 + RET_QK_DIM, :] = ((t1 * sin_t + t2 * cos_t) * k_scale).astype(BF16)
    v0 = hd * RET_V_DIM
    gate = _dot(h, wg_ref[:, v0:v0 + RET_V_DIM])
    sg_ref[:, v0:v0 + RET_V_DIM] = (gate * jax.nn.sigmoid(gate)).astype(BF16)
    v_ref[:, v0:v0 + RET_V_DIM] = _dot(h, wv_ref[:, v0:v0 + RET_V_DIM]).astype(BF16)


def _retention_head(hd, q_ref, kt_ref, v_ref, sg_ref, w_ref, qd_ref, kd_ref, o_ref, state_ref, block_decay):
    r = RET_BLOCK
    qk_cols = slice(hd * RET_QK_DIM, (hd + 1) * RET_QK_DIM)
    v_cols = slice(hd * RET_V_DIM, (hd + 1) * RET_V_DIM)
    for blk in range(q_ref.shape[0] // r):
        rows = slice(blk * r, (blk + 1) * r)
        q = q_ref[rows, qk_cols]
        kt = kt_ref[qk_cols, rows]
        v = v_ref[rows, v_cols]
        state = state_ref[hd]
        p = (_dot(q, kt) * w_ref[hd]).astype(BF16)
        q_dec = (q.astype(F32) * qd_ref[hd]).astype(BF16)
        y = _dot(jnp.concatenate([p, q_dec], axis=1), jnp.concatenate([v, state.astype(BF16)], axis=0))
        kt_dec = (kt.astype(F32) * kd_ref[hd]).astype(BF16)
        state_ref[hd] = state * block_decay[hd] + _dot(kt_dec, v)

        mu = jnp.mean(y, axis=-1, keepdims=True)
        yc = y - mu
        var = jnp.mean(yc * yc, axis=-1, keepdims=True)
        normed = (yc * lax.rsqrt(var + EPS)).astype(BF16)
        o_ref[rows, v_cols] = sg_ref[rows, v_cols] * normed


def _retention_tables():
    r = RET_BLOCK
    gamma = 1.0 - 2.0 ** (-5.0 - np.arange(RET_HEADS, dtype=np.float64))
    pos = np.arange(r)
    diff = (pos[:, None] - pos[None, :]).astype(np.float64)
    same = (pos[:, None] // CHUNK) == (pos[None, :] // CHUNK)
    earlier = (pos[None, :] // CHUNK) < (pos[:, None] // CHUNK)
    expo = np.where(same, np.abs(diff), diff)
    w = np.where((same | earlier)[None], gamma[:, None, None] ** expo[None], 0.0)
    idx = np.arange(r, dtype=np.float64)
    qd = gamma[:, None] ** (idx[None, :] + 1.0)
    qd = np.broadcast_to(qd[..., None], (RET_HEADS, r, RET_QK_DIM))
    kd = (gamma[:, None] ** (r - 1.0 - idx[None, :]))[:, None, :]
    block_decay = tuple(float(g) ** r for g in gamma)
    f32 = lambda a: np.ascontiguousarray(a, dtype=np.float32)
    return f32(w), f32(qd), f32(kd), block_decay


def _ret_layer_kernel(x_ref, g_ref, wq_ref, wk_ref, wv_ref, wg_ref, cos_ref, sin_ref, cost_ref, sint_ref,
                      w_ref, qd_ref, kd_ref, o_ref, q_s, kt_s, v_s, sg_s, state_ref,
                      *, tiles_per_seq, block_decay):
    @pl.when(pl.program_id(0) % tiles_per_seq == 0)
    def _():
        state_ref[...] = jnp.zeros_like(state_ref)

    h = _rmsnorm(x_ref[...], g_ref[...]).astype(BF16)
    project = functools.partial(_ret_project_head, h=h, wq_ref=wq_ref, wk_ref=wk_ref, wv_ref=wv_ref,
                                wg_ref=wg_ref, cos_ref=cos_ref, sin_ref=sin_ref, cost_ref=cost_ref,
                                sint_ref=sint_ref, q_ref=q_s, kt_ref=kt_s, v_ref=v_s, sg_ref=sg_s)
    retain = functools.partial(_retention_head, q_ref=q_s, kt_ref=kt_s, v_ref=v_s, sg_ref=sg_s, w_ref=w_ref,
                               qd_ref=qd_ref, kd_ref=kd_ref, o_ref=o_ref, state_ref=state_ref,
                               block_decay=block_decay)
    project(0)
    for hd in range(1, RET_HEADS):
        project(hd)
        retain(hd - 1)
    retain(RET_HEADS - 1)


def _ret_layer(x2d, g, w, cos, sin, seq):
    t = x2d.shape[0]
    tm = ROW_TILE
    r = RET_BLOCK
    tiles_per_seq = seq // tm
    half = RET_QK_DIM // 2
    mask, qd, kd, block_decay = _retention_tables()
    row = lambda i: (i, 0)
    const = lambda i: (0, 0)
    const3 = lambda i: (0, 0, 0)
    return pl.pallas_call(
        functools.partial(_ret_layer_kernel, tiles_per_seq=tiles_per_seq, block_decay=block_decay),
        grid=(t // tm,),
        in_specs=[
            pl.BlockSpec((tm, D_MODEL), row),
            pl.BlockSpec((1, D_MODEL), const),
            pl.BlockSpec((D_MODEL, D_MODEL), const, pipeline_mode=pl.Buffered(1)),
            pl.BlockSpec((D_MODEL, D_MODEL), lambda i: (0, 1), pipeline_mode=pl.Buffered(1)),
            pl.BlockSpec((D_MODEL, 2 * D_MODEL), lambda i: (0, 1), pipeline_mode=pl.Buffered(1)),
            pl.BlockSpec((D_MODEL, 2 * D_MODEL), lambda i: (0, 2), pipeline_mode=pl.Buffered(1)),
            pl.BlockSpec((tm, half), lambda i: (i % tiles_per_seq, 0)),
            pl.BlockSpec((tm, half), lambda i: (i % tiles_per_seq, 0)),
            pl.BlockSpec((half, tm), lambda i: (0, i % tiles_per_seq)),
            pl.BlockSpec((half, tm), lambda i: (0, i % tiles_per_seq)),
            pl.BlockSpec((RET_HEADS, r, r), const3),
            pl.BlockSpec((RET_HEADS, r, RET_QK_DIM), const3),
            pl.BlockSpec((RET_HEADS, 1, r), const3),
        ],
        out_specs=pl.BlockSpec((tm, 2 * D_MODEL), row),
        out_shape=jax.ShapeDtypeStruct((t, 2 * D_MODEL), BF16),
        scratch_shapes=[
            pltpu.VMEM((tm, D_MODEL), BF16),
            pltpu.VMEM((D_MODEL, tm), BF16),
            pltpu.VMEM((tm, 2 * D_MODEL), BF16),
            pltpu.VMEM((tm, 2 * D_MODEL), BF16),
            pltpu.VMEM((RET_HEADS, RET_QK_DIM, RET_V_DIM), F32),
        ],
        compiler_params=_params("arbitrary"),
        name="ret_layer",
    )(x2d, g, w, w, w, w, cos, sin, np.ascontiguousarray(cos.T), np.ascontiguousarray(sin.T), mask, qd, kd)


def kernel(x, norm_mix, norm_ffn, fox_w_in, fox_b_f, fox_w_out, ret_w_in, ret_w_out,
           ffn_w_in, ffn_w_out, final_norm):
    batch, seq, d = x.shape
    t = batch * seq
    x2d = x.reshape(t, d)
    row = lambda a: a.reshape(1, -1)

    w_in = fox_w_in[0]
    w_qkv = w_in[:, :3 * D_MODEL].astype(BF16)
    pad = LANES - C_PARTS * FOX_HEADS
    wf = jnp.pad(jnp.tile(w_in[:, 3 * D_MODEL:], (1, C_PARTS)), ((0, 0), (0, pad))).astype(BF16)
    bf = jnp.pad(jnp.tile(fox_b_f[0], C_PARTS), (0, pad)).reshape(1, LANES)
    k, qvt, ca = _fox_proj(x2d, row(norm_mix[0]), w_qkv, wf, bf, seq)
    attn = _fox_attn(k, ca, qvt, batch, seq)
    ffn_in = ffn_w_in.astype(BF16)
    ffn_out = ffn_w_out.astype(BF16)
    x2d = _ffn(attn, fox_w_out[0].astype(BF16), x2d, row(norm_ffn[0]), ffn_in, ffn_out, 0,
               row(final_norm), False, "ffn0")

    half = RET_QK_DIM // 2
    inv = ROPE_BASE ** (-np.arange(half, dtype=np.float64) / half)
    ang = np.arange(seq, dtype=np.float64)[:, None] * inv[None, :]
    cos, sin = np.cos(ang).astype(np.float32), np.sin(ang).astype(np.float32)
    w_in = ret_w_in[0]
    gated = _ret_layer(x2d, row(norm_mix[1]), w_in.astype(BF16), cos, sin, seq)
    x2d = _ffn(gated, ret_w_out[0].astype(BF16), x2d, row(norm_ffn[1]), ffn_in, ffn_out, 1,
               row(final_norm), True, "ffn1")
    return x2d.reshape(batch, seq, d)
```

```python
import functools
import math

import jax
import jax.numpy as jnp
import numpy as np
from jax import lax
from jax.experimental import pallas as pl
from jax.experimental.pallas import tpu as pltpu

D_MODEL = 1024
CHUNK = 64
FOX_HEADS = 8
FOX_HEAD_DIM = D_MODEL // FOX_HEADS
RET_HEADS = 4
RET_QK_DIM = D_MODEL // RET_HEADS
RET_V_DIM = 2 * D_MODEL // RET_HEADS
D_FF = -(-8 * D_MODEL // (3 * 256)) * 256
ROPE_BASE = 10000.0
EPS = 1e-6
LOG2E = math.log2(math.e)

F32 = jnp.float32
BF16 = jnp.bfloat16

VMEM_LIMIT_BYTES = 56 * 1024 * 1024
LANES = 128

ROW_TILE = 512
ATTN_Q_TILE = 2 * ROW_TILE
ATTN_K_TILE = ATTN_Q_TILE // 2
RET_BLOCK = 256
FFN_CHUNKS = ((0, 1024), (1024, 1024), (2048, 768))
RET_FFN_CHUNKS = tuple((c0, min(512, D_FF - c0)) for c0 in range(0, D_FF, 512))
C_PARTS = 3
ONES_ROWS = 16


def _params(*semantics):
    return pltpu.CompilerParams(dimension_semantics=semantics, vmem_limit_bytes=VMEM_LIMIT_BYTES)


def _rmsnorm(x, g):
    return x * lax.rsqrt(jnp.mean(x * x, axis=-1, keepdims=True) + EPS) * g


def _dot(a, b):
    return jnp.dot(a, b, preferred_element_type=F32)


def _cumsum_rows(v):
    n = v.shape[0]
    row = lax.broadcasted_iota(jnp.int32, v.shape, 0)
    shift = 1
    while shift < n:
        rolled = pltpu.roll(v, shift, axis=0)
        v = v + jnp.where(row >= shift, rolled, 0.0)
        shift *= 2
    return v


def _fox_proj_kernel(x_ref, g_ref, wq_ref, wk_ref, wv_ref, wf_ref, bf_ref, k_ref, qvt_ref, ca_ref, carry_ref,
                     *, tiles_per_seq):
    i = pl.program_id(0)
    tm = x_ref.shape[0]
    h = _rmsnorm(x_ref[...], g_ref[...]).astype(BF16)
    logits = _dot(h, wf_ref[...]) + bf_ref[...]
    log_f = jnp.minimum(logits, 0.0) - jnp.log1p(jnp.exp(-jnp.abs(logits)))
    carry = jnp.where(i % tiles_per_seq == 0, 0.0, carry_ref[...])
    c = _cumsum_rows(log_f) + carry
    carry_ref[...] = c[tm - 1:tm, :]
    neg_c = c * (-LOG2E)
    hi = neg_c.astype(BF16)
    rest = neg_c - hi.astype(F32)
    mid = rest.astype(BF16)
    lo = (rest - mid.astype(F32)).astype(BF16)
    lane = lax.broadcasted_iota(jnp.int32, neg_c.shape, 1)
    ca_ref[...] = jnp.where(lane < FOX_HEADS, hi, jnp.where(lane < 2 * FOX_HEADS, mid, lo))

    q_scale = FOX_HEAD_DIM ** -0.5 * LOG2E
    tn = 512
    for n0 in range(0, D_MODEL, tn):
        acc = _dot(h, wk_ref[:, n0:n0 + tn]).astype(BF16)
        for d0 in range(0, tn, FOX_HEAD_DIM):
            k_ref[(n0 + d0) // FOX_HEAD_DIM] = acc[:, d0:d0 + FOX_HEAD_DIM]
    for n0 in range(0, D_MODEL, tn):
        acc = jnp.transpose(_dot(h, wq_ref[:, n0:n0 + tn])) * q_scale
        qvt_ref[n0:n0 + tn, :] = acc.astype(BF16)
    for n0 in range(0, D_MODEL, tn):
        acc = jnp.transpose(_dot(h, wv_ref[:, n0:n0 + tn]))
        qvt_ref[D_MODEL + n0:D_MODEL + n0 + tn, :] = acc.astype(BF16)


def _fox_proj(x2d, g, w_qkv, wf, bf, seq):
    t = x2d.shape[0]
    tm = ROW_TILE
    window = lambda col: pl.BlockSpec((D_MODEL, D_MODEL), lambda i: (0, col), pipeline_mode=pl.Buffered(1))
    return pl.pallas_call(
        functools.partial(_fox_proj_kernel, tiles_per_seq=seq // tm),
        grid=(t // tm,),
        in_specs=[
            pl.BlockSpec((tm, D_MODEL), lambda i: (i, 0)),
            pl.BlockSpec((1, D_MODEL), lambda i: (0, 0)),
            window(0),
            window(1),
            window(2),
            pl.BlockSpec((D_MODEL, LANES), lambda i: (0, 0)),
            pl.BlockSpec((1, LANES), lambda i: (0, 0)),
        ],
        out_specs=[
            pl.BlockSpec((FOX_HEADS, tm, FOX_HEAD_DIM), lambda i: (0, i, 0)),
            pl.BlockSpec((None, 2 * D_MODEL, tm), lambda i: (i, 0, 0)),
            pl.BlockSpec((tm, LANES), lambda i: (i, 0)),
        ],
        out_shape=[
            jax.ShapeDtypeStruct((FOX_HEADS, t, FOX_HEAD_DIM), BF16),
            jax.ShapeDtypeStruct((t // tm, 2 * D_MODEL, tm), BF16),
            jax.ShapeDtypeStruct((t, LANES), BF16),
        ],
        scratch_shapes=[pltpu.VMEM((1, LANES), F32)],
        compiler_params=_params("arbitrary"),
        name="fox_proj",
    )(x2d, g, w_qkv, w_qkv, w_qkv, wf, bf)


def _fox_attn_kernel(qt_ref, k_ref, ca_ref, vt_ref, o_ref, s_ref, smax_ref, m_ref, acc_ref, done_ref):
    head = pl.program_id(1)
    tq = ATTN_Q_TILE
    tk = ATTN_K_TILE
    tv = vt_ref.shape[2]
    nq = (qt_ref.shape[0] * tv) // tq
    assert tq == 2 * tk and tk == tv

    piece = lax.broadcasted_iota(jnp.int32, (LANES, tq), 0)
    pick = jnp.logical_and(piece % FOX_HEADS == head, piece < C_PARTS * FOX_HEADS)
    ones_rows = jnp.where(pick, 1.0, 0.0).astype(BF16)

    def queries(i):
        return jnp.concatenate([jnp.concatenate([qt_ref[2 * i], qt_ref[2 * i + 1]], axis=1), ones_rows], axis=0)

    def scores(qt_ext, row0, c0=0):
        k_ext = jnp.concatenate([k_ref[pl.ds(row0, tk), :], ca_ref[pl.ds(row0, tk), :]], axis=1)
        return _dot(k_ext, qt_ext[:, c0:])

    def update(s, vt, c0=0, diagonal=False, tile_max=None):
        if diagonal:
            key = lax.broadcasted_iota(jnp.int32, s.shape, 0)
            qry = lax.broadcasted_iota(jnp.int32, s.shape, 1)
            s = jnp.where(qry >= key, s, -jnp.inf)
        if tile_max is None:
            tile_max = jnp.max(s, axis=0, keepdims=True)
        m_old = m_ref[:, c0:]
        m_new = jnp.maximum(m_old, tile_max)
        alpha = jnp.exp2(m_old - m_new)
        p = jnp.exp2(s - m_new).astype(BF16)
        vt_ext = jnp.concatenate([vt, jnp.ones((ONES_ROWS, vt.shape[1]), BF16)], axis=0)
        acc_ref[:, c0:] = alpha * acc_ref[:, c0:] + _dot(vt_ext, p)
        m_ref[:, c0:] = m_new

    def stash(slot, qt_ext, row0):
        s = scores(qt_ext, row0)
        s_ref[slot] = s
        smax_ref[slot] = jnp.max(s, axis=0, keepdims=True)

    def query_tile(i, carry):
        qt_ext = queries(i)
        m_ref[...] = jnp.full_like(m_ref, -jnp.inf)
        acc_ref[...] = jnp.zeros_like(acc_ref)

        def body(j, carry):
            base = pl.multiple_of(j * tq, tq)
            stash(1, qt_ext, base + tk)
            update(s_ref[0], vt_ref[2 * j], tile_max=smax_ref[0])
            stash(0, qt_ext, base + tq)
            update(s_ref[1], vt_ref[2 * j + 1], tile_max=smax_ref[1])
            return carry

        def two_steps(jj, carry):
            body(2 * jj, carry)
            return body(2 * jj + 1, carry)

        lax.fori_loop(0, i // 2, two_steps, 0)

        @pl.when(i % 2 == 1)
        def _():
            body(i - 1, 0)

        base = pl.multiple_of(i * tq, tq)
        s_last = scores(qt_ext, base + tk, tk)
        finish(pl.multiple_of(jnp.maximum(i - 1, 0) * tq, tq))
        update(s_ref[0], vt_ref[2 * i], 0, diagonal=True)
        qt_next = queries(jnp.minimum(i + 1, nq - 1))
        stash(0, qt_next, 0)
        update(s_last, vt_ref[2 * i + 1], tk, diagonal=True)
        done_ref[...] = acc_ref[...]
        return carry

    def finish(row0):
        done = done_ref[...]
        out = done[:FOX_HEAD_DIM] / done[FOX_HEAD_DIM:FOX_HEAD_DIM + 1]
        o_ref[pl.ds(row0, tq), :] = jnp.transpose(out).astype(o_ref.dtype)

    done_ref[...] = jnp.ones_like(done_ref)
    stash(0, queries(0), 0)
    lax.fori_loop(0, nq, query_tile, 0)
    finish((nq - 1) * tq)


def _fox_attn(k, ca, qvt, batch, seq):
    t = k.shape[1]
    tq = ATTN_Q_TILE
    tk = ATTN_K_TILE
    tv = qvt.shape[2]
    hb = D_MODEL // FOX_HEAD_DIM
    return pl.pallas_call(
        _fox_attn_kernel,
        grid=(batch, FOX_HEADS),
        in_specs=[
            pl.BlockSpec((seq // tv, FOX_HEAD_DIM, tv), lambda b, h: (b, h, 0)),
            pl.BlockSpec((None, seq, FOX_HEAD_DIM), lambda b, h: (h, b, 0)),
            pl.BlockSpec((seq, LANES), lambda b, h: (b, 0)),
            pl.BlockSpec((seq // tv, FOX_HEAD_DIM, tv), lambda b, h: (b, hb + h, 0)),
        ],
        out_specs=pl.BlockSpec((None, seq, FOX_HEAD_DIM), lambda b, h: (h, b, 0)),
        out_shape=jax.ShapeDtypeStruct((FOX_HEADS, t, FOX_HEAD_DIM), BF16),
        scratch_shapes=[
            pltpu.VMEM((2, tk, tq), F32),
            pltpu.VMEM((2, 1, tq), F32),
            pltpu.VMEM((1, tq), F32),
            pltpu.VMEM((FOX_HEAD_DIM + ONES_ROWS, tq), F32),
            pltpu.VMEM((FOX_HEAD_DIM + ONES_ROWS, tq), F32),
        ],
        compiler_params=_params("arbitrary", "arbitrary"),
        name="fox_attn",
    )(qvt, k, ca, qvt)


def _ffn_body(mixed, wp_ref, x_ref, g_ref, wi_ref, wo_ref, fg_ref, o_ref, final_norm,
              chunks=FFN_CHUNKS, between=()):
    between = list(between)
    gaps = [1 + 3 * len(chunks)]

    def fill_gap():
        for _ in range(-(-len(between) // gaps[0])):
            between.pop(0)()
        gaps[0] -= 1

    x = x_ref[...] + _dot(mixed, wp_ref[...])
    h = _rmsnorm(x, g_ref[...]).astype(BF16)
    out = x
    fill_gap()
    for c0, width in chunks:
        gate = _dot(h, wi_ref[:, c0:c0 + width])
        fill_gap()
        up = _dot(h, wi_ref[:, D_FF + c0:D_FF + c0 + width])
        fill_gap()
        act = (gate * jax.nn.sigmoid(gate) * up).astype(BF16)
        out = out + _dot(act, wo_ref[c0:c0 + width, :])
        fill_gap()
    assert not between
    if final_norm:
        out = _rmsnorm(out, fg_ref[...])
    o_ref[...] = out


def _ffn_kernel(a_ref, wp_ref, x_ref, g_ref, wi_ref, wo_ref, fg_ref, o_ref, *, final_norm):
    mixed = jnp.concatenate([a_ref[hd] for hd in range(a_ref.shape[0])], axis=1)
    _ffn_body(mixed, wp_ref, x_ref, g_ref, wi_ref, wo_ref, fg_ref, o_ref, final_norm)


def _ret_ffn_kernel(q_ref, kt_ref, v_ref, sg_ref, w_ref, qd_ref, kd_ref,
                    wp_ref, x_ref, g_ref, wi_ref, wo_ref, fg_ref, o_ref, gated_ref, state_ref,
                    *, tiles_per_seq, block_decay):
    i = pl.program_id(0)

    @pl.when(i == 0)
    def _():
        gated_ref[...] = jnp.zeros_like(gated_ref)

    @pl.when(i % tiles_per_seq == 0)
    def _():
        state_ref[...] = jnp.zeros_like(state_ref)

    stages = _retention_stages(q_ref, kt_ref, v_ref, sg_ref, w_ref, qd_ref, kd_ref, gated_ref, state_ref,
                               block_decay)
    _ffn_body(gated_ref[...], wp_ref, x_ref, g_ref, wi_ref, wo_ref, fg_ref, o_ref, True,
              chunks=RET_FFN_CHUNKS, between=stages)


def _ffn_specs(w_proj, layer, rows=lambda i: (i, 0)):
    tm = ROW_TILE
    return [
        pl.BlockSpec(w_proj.shape, lambda i: (0, 0), pipeline_mode=pl.Buffered(1)),
        pl.BlockSpec((tm, D_MODEL), rows),
        pl.BlockSpec((1, D_MODEL), lambda i: (0, 0)),
        pl.BlockSpec((None, D_MODEL, 2 * D_FF), lambda i: (layer, 0, 0), pipeline_mode=pl.Buffered(1)),
        pl.BlockSpec((None, D_FF, D_MODEL), lambda i: (layer, 0, 0), pipeline_mode=pl.Buffered(1)),
        pl.BlockSpec((1, D_MODEL), lambda i: (0, 0)),
    ]


def _ffn(mixed, w_proj, x2d, g, w_in_all, w_out_all, layer, final_g, final_norm, name):
    t = x2d.shape[0]
    tm = ROW_TILE
    return pl.pallas_call(
        functools.partial(_ffn_kernel, final_norm=final_norm),
        grid=(t // tm,),
        in_specs=[pl.BlockSpec((mixed.shape[0], tm, mixed.shape[2]), lambda i: (0, i, 0))]
        + _ffn_specs(w_proj, layer),
        out_specs=pl.BlockSpec((tm, D_MODEL), lambda i: (i, 0)),
        out_shape=jax.ShapeDtypeStruct((t, D_MODEL), F32),
        compiler_params=_params("arbitrary"),
        name=name,
    )(mixed, w_proj, x2d, g, w_in_all, w_out_all, final_g)


def _ret_ffn(q, kt, v, sg, w_proj, x2d, g, w_in_all, w_out_all, layer, final_g, seq):
    t = x2d.shape[0]
    tm = ROW_TILE
    r = RET_BLOCK
    n = t // tm
    mask, qd, kd, block_decay = _retention_tables()
    retained = lambda i: (jnp.minimum(i, n - 1), 0)
    finished = lambda i: (jnp.maximum(i - 1, 0), 0)
    const3 = lambda i: (0, 0, 0)
    once = dict(pipeline_mode=pl.Buffered(1))
    return pl.pallas_call(
        functools.partial(_ret_ffn_kernel, tiles_per_seq=seq // tm, block_decay=block_decay),
        grid=(n + 1,),
        in_specs=[
            pl.BlockSpec((tm, D_MODEL), retained),
            pl.BlockSpec((None, D_MODEL, tm), lambda i: (jnp.minimum(i, n - 1), 0, 0)),
            pl.BlockSpec((tm, 2 * D_MODEL), retained),
            pl.BlockSpec((tm, 2 * D_MODEL), retained),
            pl.BlockSpec((RET_HEADS, r, r), const3, **once),
            pl.BlockSpec((RET_HEADS, r, RET_QK_DIM), const3, **once),
            pl.BlockSpec((RET_HEADS, 1, r), const3, **once),
        ]
        + _ffn_specs(w_proj, layer, finished),
        out_specs=pl.BlockSpec((tm, D_MODEL), finished),
        out_shape=jax.ShapeDtypeStruct((t, D_MODEL), F32),
        scratch_shapes=[
            pltpu.VMEM((tm, 2 * D_MODEL), BF16),
            pltpu.VMEM((RET_HEADS, RET_QK_DIM, RET_V_DIM), F32),
        ],
        compiler_params=_params("arbitrary"),
        name="ret_ffn",
    )(q, kt, v, sg, mask, qd, kd, w_proj, x2d, g, w_in_all, w_out_all, final_g)


def _ret_project_head(hd, h, wq_ref, wk_ref, wv_ref, wg_ref, cos_ref, sin_ref, cost_ref, sint_ref,
                      q_ref, kt_ref, v_ref, sg_ref):
    half = RET_QK_DIM // 2
    c0 = hd * RET_QK_DIM
    cos, sin = cos_ref[...], sin_ref[...]
    acc = _dot(h, wq_ref[:, c0:c0 + RET_QK_DIM])
    t1, t2 = acc[:, :half], acc[:, half:]
    q_ref[:, c0:c0 + half] = (t1 * cos - t2 * sin).astype(BF16)
    q_ref[:, c0 + half:c0 + RET_QK_DIM] = (t1 * sin + t2 * cos).astype(BF16)
    cos_t, sin_t = cost_ref[...], sint_ref[...]
    k_scale = RET_QK_DIM ** -0.5
    acc = jnp.transpose(_dot(h, wk_ref[:, c0:c0 + RET_QK_DIM]))
    t1, t2 = acc[:half], acc[half:]
    kt_ref[c0:c0 + half, :] = ((t1 * cos_t - t2 * sin_t) * k_scale).astype(BF16)
    kt_ref[c0 + half:c0 + RET_QK_DIM, :] = ((t1 * sin_t + t2 * cos_t) * k_scale).astype(BF16)
    v0 = hd * RET_V_DIM
    gate = _dot(h, wg_ref[:, v0:v0 + RET_V_DIM])
    sg_ref[:, v0:v0 + RET_V_DIM] = (gate * jax.nn.sigmoid(gate)).astype(BF16)
    v_ref[:, v0:v0 + RET_V_DIM] = _dot(h, wv_ref[:, v0:v0 + RET_V_DIM]).astype(BF16)


def _retention_stages(q_ref, kt_ref, v_ref, sg_ref, w_ref, qd_ref, kd_ref, o_ref, state_ref, block_decay):
    r = RET_BLOCK
    stages = []
    for hd in range(RET_HEADS):
        qk_cols = slice(hd * RET_QK_DIM, (hd + 1) * RET_QK_DIM)
        v_cols = slice(hd * RET_V_DIM, (hd + 1) * RET_V_DIM)
        for blk in range(q_ref.shape[0] // r):
            rows = slice(blk * r, (blk + 1) * r)
            held = {}

            def score(hd=hd, rows=rows, qk_cols=qk_cols, held=held):
                q = q_ref[rows, qk_cols]
                kt = kt_ref[qk_cols, rows]
                held["p"] = (_dot(q, kt) * w_ref[hd]).astype(BF16)
                held["q_dec"] = (q.astype(F32) * qd_ref[hd]).astype(BF16)
                held["kt_dec"] = (kt.astype(F32) * kd_ref[hd]).astype(BF16)

            def mix(hd=hd, rows=rows, v_cols=v_cols, held=held):
                v = v_ref[rows, v_cols]
                state = state_ref[hd]
                held["y"] = _dot(jnp.concatenate([held["p"], held["q_dec"]], axis=1),
                                 jnp.concatenate([v, state.astype(BF16)], axis=0))
                state_ref[hd] = state * block_decay[hd] + _dot(held["kt_dec"], v)

            def norm(rows=rows, v_cols=v_cols, held=held):
                y = held["y"]
                mu = jnp.mean(y, axis=-1, keepdims=True)
                yc = y - mu
                var = jnp.mean(yc * yc, axis=-1, keepdims=True)
                normed = (yc * lax.rsqrt(var + EPS)).astype(BF16)
                o_ref[rows, v_cols] = sg_ref[rows, v_cols] * normed

            stages += [score, mix, norm]
    return stages


def _retention_tables():
    r = RET_BLOCK
    gamma = 1.0 - 2.0 ** (-5.0 - np.arange(RET_HEADS, dtype=np.float64))
    pos = np.arange(r)
    diff = (pos[:, None] - pos[None, :]).astype(np.float64)
    same = (pos[:, None] // CHUNK) == (pos[None, :] // CHUNK)
    earlier = (pos[None, :] // CHUNK) < (pos[:, None] // CHUNK)
    expo = np.where(same, np.abs(diff), diff)
    w = np.where((same | earlier)[None], gamma[:, None, None] ** expo[None], 0.0)
    idx = np.arange(r, dtype=np.float64)
    qd = gamma[:, None] ** (idx[None, :] + 1.0)
    qd = np.broadcast_to(qd[..., None], (RET_HEADS, r, RET_QK_DIM))
    kd = (gamma[:, None] ** (r - 1.0 - idx[None, :]))[:, None, :]
    block_decay = tuple(float(g) ** r for g in gamma)
    f32 = lambda a: np.ascontiguousarray(a, dtype=np.float32)
    return f32(w), f32(qd), f32(kd), block_decay


def _ret_proj_kernel(x_ref, g_ref, wq_ref, wk_ref, wv_ref, wg_ref, cos_ref, sin_ref, cost_ref, sint_ref,
                     q_ref, kt_ref, v_ref, sg_ref):
    h = _rmsnorm(x_ref[...], g_ref[...]).astype(BF16)
    for hd in range(RET_HEADS):
        _ret_project_head(hd, h, wq_ref, wk_ref, wv_ref, wg_ref, cos_ref, sin_ref, cost_ref, sint_ref,
                          q_ref, kt_ref, v_ref, sg_ref)


def _ret_proj(x2d, g, w, cos, sin, seq):
    t = x2d.shape[0]
    tm = ROW_TILE
    tiles_per_seq = seq // tm
    half = RET_QK_DIM // 2
    row = lambda i: (i, 0)
    const = lambda i: (0, 0)
    return pl.pallas_call(
        _ret_proj_kernel,
        grid=(t // tm,),
        in_specs=[
            pl.BlockSpec((tm, D_MODEL), row),
            pl.BlockSpec((1, D_MODEL), const),
            pl.BlockSpec((D_MODEL, D_MODEL), const, pipeline_mode=pl.Buffered(1)),
            pl.BlockSpec((D_MODEL, D_MODEL), lambda i: (0, 1), pipeline_mode=pl.Buffered(1)),
            pl.BlockSpec((D_MODEL, 2 * D_MODEL), lambda i: (0, 1), pipeline_mode=pl.Buffered(1)),
            pl.BlockSpec((D_MODEL, 2 * D_MODEL), lambda i: (0, 2), pipeline_mode=pl.Buffered(1)),
            pl.BlockSpec((tm, half), lambda i: (i % tiles_per_seq, 0)),
            pl.BlockSpec((tm, half), lambda i: (i % tiles_per_seq, 0)),
            pl.BlockSpec((half, tm), lambda i: (0, i % tiles_per_seq)),
            pl.BlockSpec((half, tm), lambda i: (0, i % tiles_per_seq)),
        ],
        out_specs=[
            pl.BlockSpec((tm, D_MODEL), row),
            pl.BlockSpec((None, D_MODEL, tm), lambda i: (i, 0, 0)),
            pl.BlockSpec((tm, 2 * D_MODEL), row),
            pl.BlockSpec((tm, 2 * D_MODEL), row),
        ],
        out_shape=[
            jax.ShapeDtypeStruct((t, D_MODEL), BF16),
            jax.ShapeDtypeStruct((t // tm, D_MODEL, tm), BF16),
            jax.ShapeDtypeStruct((t, 2 * D_MODEL), BF16),
            jax.ShapeDtypeStruct((t, 2 * D_MODEL), BF16),
        ],
        compiler_params=_params("arbitrary"),
        name="ret_proj",
    )(x2d, g, w, w, w, w, cos, sin, np.ascontiguousarray(cos.T), np.ascontiguousarray(sin.T))


def kernel(x, norm_mix, norm_ffn, fox_w_in, fox_b_f, fox_w_out, ret_w_in, ret_w_out,
           ffn_w_in, ffn_w_out, final_norm):
    batch, seq, d = x.shape
    t = batch * seq
    x2d = x.reshape(t, d)
    row = lambda a: a.reshape(1, -1)

    w_in = fox_w_in[0]
    w_qkv = w_in.astype(BF16)
    pad = LANES - C_PARTS * FOX_HEADS
    wf = jnp.pad(jnp.tile(w_in[:, 3 * D_MODEL:], (1, C_PARTS)), ((0, 0), (0, pad))).astype(BF16)
    bf = jnp.pad(jnp.tile(fox_b_f[0], C_PARTS), (0, pad)).reshape(1, LANES)
    k, qvt, ca = _fox_proj(x2d, row(norm_mix[0]), w_qkv, wf, bf, seq)
    attn = _fox_attn(k, ca, qvt, batch, seq)
    ffn_in = ffn_w_in.astype(BF16)
    ffn_out = ffn_w_out.astype(BF16)
    x2d = _ffn(attn, fox_w_out[0].astype(BF16), x2d, row(norm_ffn[0]), ffn_in, ffn_out, 0,
               row(final_norm), False, "ffn0")

    half = RET_QK_DIM // 2
    inv = ROPE_BASE ** (-np.arange(half, dtype=np.float64) / half)
    ang = np.arange(seq, dtype=np.float64)[:, None] * inv[None, :]
    cos, sin = np.cos(ang).astype(np.float32), np.sin(ang).astype(np.float32)
    w_in = ret_w_in[0]
    q, kt, v, sg = _ret_proj(x2d, row(norm_mix[1]), w_in.astype(BF16), cos, sin, seq)
    x2d = _ret_ffn(q, kt, v, sg, ret_w_out[0].astype(BF16), x2d, row(norm_ffn[1]), ffn_in, ffn_out, 1,
                   row(final_norm), seq)
    return x2d.reshape(batch, seq, d)
```

```python
import functools
import math

import jax
import jax.numpy as jnp
import numpy as np
from jax import lax
from jax.experimental import pallas as pl
from jax.experimental.pallas import tpu as pltpu

D_MODEL = 1024
CHUNK = 64
FOX_HEADS = 8
FOX_HEAD_DIM = D_MODEL // FOX_HEADS
RET_HEADS = 4
RET_QK_DIM = D_MODEL // RET_HEADS
RET_V_DIM = 2 * D_MODEL // RET_HEADS
D_FF = -(-8 * D_MODEL // (3 * 256)) * 256
ROPE_BASE = 10000.0
EPS = 1e-6
LOG2E = math.log2(math.e)

F32 = jnp.float32
BF16 = jnp.bfloat16

VMEM_LIMIT_BYTES = 56 * 1024 * 1024
LANES = 128

ROW_TILE = 512
ATTN_Q_TILE = 2 * ROW_TILE
ATTN_K_TILE = ATTN_Q_TILE // 2
RET_BLOCK = 256
FFN_CHUNKS = ((0, 1024), (1024, 1024), (2048, 768))
RET_FFN_CHUNKS = tuple((c0, min(512, D_FF - c0)) for c0 in range(0, D_FF, 512))
C_PARTS = 3
ONES_ROWS = 16


def _params(*semantics):
    return pltpu.CompilerParams(dimension_semantics=semantics, vmem_limit_bytes=VMEM_LIMIT_BYTES)


def _rmsnorm(x, g):
    return x * lax.rsqrt(jnp.mean(x * x, axis=-1, keepdims=True) + EPS) * g


def _dot(a, b):
    return jnp.dot(a, b, preferred_element_type=F32)


def _cumsum_rows(v):
    n = v.shape[0]
    row = lax.broadcasted_iota(jnp.int32, v.shape, 0)
    shift = 1
    while shift < n:
        rolled = pltpu.roll(v, shift, axis=0)
        v = v + jnp.where(row >= shift, rolled, 0.0)
        shift *= 2
    return v


def _fox_proj_kernel(x_ref, g_ref, wq_ref, wk_ref, wv_ref, wf_ref, bf_ref, k_ref, qvt_ref, ca_ref, carry_ref,
                     *, tiles_per_seq):
    i = pl.program_id(0)
    tm = x_ref.shape[0]
    h = _rmsnorm(x_ref[...], g_ref[...]).astype(BF16)
    logits = _dot(h, wf_ref[...]) + bf_ref[...]
    log_f = jnp.minimum(logits, 0.0) - jnp.log1p(jnp.exp(-jnp.abs(logits)))
    carry = jnp.where(i % tiles_per_seq == 0, 0.0, carry_ref[...])
    c = _cumsum_rows(log_f) + carry
    carry_ref[...] = c[tm - 1:tm, :]
    neg_c = c * (-LOG2E)
    hi = neg_c.astype(BF16)
    rest = neg_c - hi.astype(F32)
    mid = rest.astype(BF16)
    lo = (rest - mid.astype(F32)).astype(BF16)
    lane = lax.broadcasted_iota(jnp.int32, neg_c.shape, 1)
    ca_ref[...] = jnp.where(lane < FOX_HEADS, hi, jnp.where(lane < 2 * FOX_HEADS, mid, lo))

    q_scale = FOX_HEAD_DIM ** -0.5 * LOG2E
    tn = 512
    for n0 in range(0, D_MODEL, tn):
        acc = _dot(h, wk_ref[:, n0:n0 + tn]).astype(BF16)
        for d0 in range(0, tn, FOX_HEAD_DIM):
            k_ref[(n0 + d0) // FOX_HEAD_DIM] = acc[:, d0:d0 + FOX_HEAD_DIM]
    for n0 in range(0, D_MODEL, tn):
        acc = jnp.transpose(_dot(h, wq_ref[:, n0:n0 + tn])) * q_scale
        qvt_ref[n0:n0 + tn, :] = acc.astype(BF16)
    for n0 in range(0, D_MODEL, tn):
        acc = jnp.transpose(_dot(h, wv_ref[:, n0:n0 + tn]))
        qvt_ref[D_MODEL + n0:D_MODEL + n0 + tn, :] = acc.astype(BF16)


def _fox_proj(x2d, g, w_qkv, wf, bf, seq):
    t = x2d.shape[0]
    tm = ROW_TILE
    window = lambda col: pl.BlockSpec((D_MODEL, D_MODEL), lambda i: (0, col), pipeline_mode=pl.Buffered(1))
    return pl.pallas_call(
        functools.partial(_fox_proj_kernel, tiles_per_seq=seq // tm),
        grid=(t // tm,),
        in_specs=[
            pl.BlockSpec((tm, D_MODEL), lambda i: (i, 0)),
            pl.BlockSpec((1, D_MODEL), lambda i: (0, 0)),
            window(0),
            window(1),
            window(2),
            pl.BlockSpec((D_MODEL, LANES), lambda i: (0, 0)),
            pl.BlockSpec((1, LANES), lambda i: (0, 0)),
        ],
        out_specs=[
            pl.BlockSpec((FOX_HEADS, tm, FOX_HEAD_DIM), lambda i: (0, i, 0)),
            pl.BlockSpec((None, 2 * D_MODEL, tm), lambda i: (i, 0, 0)),
            pl.BlockSpec((tm, LANES), lambda i: (i, 0)),
        ],
        out_shape=[
            jax.ShapeDtypeStruct((FOX_HEADS, t, FOX_HEAD_DIM), BF16),
            jax.ShapeDtypeStruct((t // tm, 2 * D_MODEL, tm), BF16),
            jax.ShapeDtypeStruct((t, LANES), BF16),
        ],
        scratch_shapes=[pltpu.VMEM((1, LANES), F32)],
        compiler_params=_params("arbitrary"),
        name="fox_proj",
    )(x2d, g, w_qkv, w_qkv, w_qkv, wf, bf)


def _fox_attn_kernel(qt_ref, k_ref, ca_ref, vt_ref, o_ref, s_ref, smax_ref, m_ref, acc_ref, done_ref):
    head = pl.program_id(1)
    tq = ATTN_Q_TILE
    tk = ATTN_K_TILE
    tv = vt_ref.shape[2]
    nq = (qt_ref.shape[0] * tv) // tq
    assert tq == 2 * tk and tk == tv

    piece = lax.broadcasted_iota(jnp.int32, (LANES, tq), 0)
    pick = jnp.logical_and(piece % FOX_HEADS == head, piece < C_PARTS * FOX_HEADS)
    ones_rows = jnp.where(pick, 1.0, 0.0).astype(BF16)

    def queries(i):
        return jnp.concatenate([jnp.concatenate([qt_ref[2 * i], qt_ref[2 * i + 1]], axis=1), ones_rows], axis=0)

    def scores(qt_ext, row0, c0=0):
        k_ext = jnp.concatenate([k_ref[pl.ds(row0, tk), :], ca_ref[pl.ds(row0, tk), :]], axis=1)
        return _dot(k_ext, qt_ext[:, c0:])

    def update(s, vt, c0=0, diagonal=False, tile_max=None):
        if diagonal:
            key = lax.broadcasted_iota(jnp.int32, s.shape, 0)
            qry = lax.broadcasted_iota(jnp.int32, s.shape, 1)
            s = jnp.where(qry >= key, s, -jnp.inf)
        if tile_max is None:
            tile_max = jnp.max(s, axis=0, keepdims=True)
        m_old = m_ref[:, c0:]
        m_new = jnp.maximum(m_old, tile_max)
        alpha = jnp.exp2(m_old - m_new)
        p = jnp.exp2(s - m_new).astype(BF16)
        vt_ext = jnp.concatenate([vt, jnp.ones((ONES_ROWS, vt.shape[1]), BF16)], axis=0)
        acc_ref[:, c0:] = alpha * acc_ref[:, c0:] + _dot(vt_ext, p)
        m_ref[:, c0:] = m_new

    def stash(slot, qt_ext, row0):
        s = scores(qt_ext, row0)
        s_ref[slot] = s
        smax_ref[slot] = jnp.max(s, axis=0, keepdims=True)

    def query_tile(i, head_slot, next_slot, odd):
        qt_ext = queries(i)
        m_ref[...] = jnp.full_like(m_ref, -jnp.inf)
        acc_ref[...] = jnp.zeros_like(acc_ref)

        def body(j, carry):
            base = pl.multiple_of(j * tq, tq)
            stash(1, qt_ext, base + tk)
            update(s_ref[head_slot], vt_ref[2 * j], tile_max=smax_ref[head_slot])
            stash(head_slot, qt_ext, base + tq)
            update(s_ref[1], vt_ref[2 * j + 1], tile_max=smax_ref[1])
            return carry

        def two_steps(jj, carry):
            body(2 * jj, carry)
            return body(2 * jj + 1, carry)

        lax.fori_loop(0, i // 2, two_steps, 0)
        if odd:
            body(i - 1, 0)

        base = pl.multiple_of(i * tq, tq)
        s_last = scores(qt_ext, base + tk, tk)
        qt_next = queries(jnp.minimum(i + 1, nq - 1))
        stash(next_slot, qt_next, 0)
        finish(pl.multiple_of(jnp.maximum(i - 1, 0) * tq, tq))
        update(s_ref[head_slot], vt_ref[2 * i], 0, diagonal=True)
        update(s_last, vt_ref[2 * i + 1], tk, diagonal=True)
        done_ref[...] = acc_ref[...]

    def two_query_tiles(pair, carry):
        query_tile(2 * pair, 0, 2, odd=False)
        query_tile(2 * pair + 1, 2, 0, odd=True)
        return carry

    def finish(row0):
        done = done_ref[...]
        out = done[:FOX_HEAD_DIM] / done[FOX_HEAD_DIM:FOX_HEAD_DIM + 1]
        o_ref[pl.ds(row0, tq), :] = jnp.transpose(out).astype(o_ref.dtype)

    done_ref[...] = jnp.ones_like(done_ref)
    stash(0, queries(0), 0)
    assert nq % 2 == 0
    lax.fori_loop(0, nq // 2, two_query_tiles, 0)
    finish((nq - 1) * tq)


def _fox_attn(k, ca, qvt, batch, seq):
    t = k.shape[1]
    tq = ATTN_Q_TILE
    tk = ATTN_K_TILE
    tv = qvt.shape[2]
    hb = D_MODEL // FOX_HEAD_DIM
    return pl.pallas_call(
        _fox_attn_kernel,
        grid=(batch, FOX_HEADS),
        in_specs=[
            pl.BlockSpec((seq // tv, FOX_HEAD_DIM, tv), lambda b, h: (b, h, 0)),
            pl.BlockSpec((None, seq, FOX_HEAD_DIM), lambda b, h: (h, b, 0)),
            pl.BlockSpec((seq, LANES), lambda b, h: (b, 0)),
            pl.BlockSpec((seq // tv, FOX_HEAD_DIM, tv), lambda b, h: (b, hb + h, 0)),
        ],
        out_specs=pl.BlockSpec((None, seq, FOX_HEAD_DIM), lambda b, h: (h, b, 0)),
        out_shape=jax.ShapeDtypeStruct((FOX_HEADS, t, FOX_HEAD_DIM), BF16),
        scratch_shapes=[
            pltpu.VMEM((3, tk, tq), F32),
            pltpu.VMEM((3, 1, tq), F32),
            pltpu.VMEM((1, tq), F32),
            pltpu.VMEM((FOX_HEAD_DIM + ONES_ROWS, tq), F32),
            pltpu.VMEM((FOX_HEAD_DIM + ONES_ROWS, tq), F32),
        ],
        compiler_params=_params("arbitrary", "arbitrary"),
        name="fox_attn",
    )(qvt, k, ca, qvt)


def _ffn_body(mixed, wp_ref, x_ref, g_ref, wi_ref, wo_ref, fg_ref, o_ref, final_norm,
              chunks=FFN_CHUNKS, between=()):
    between = list(between)
    gaps = [1 + 3 * len(chunks)]

    def fill_gap():
        for _ in range(-(-len(between) // gaps[0])):
            between.pop(0)()
        gaps[0] -= 1

    x = x_ref[...] + _dot(mixed, wp_ref[...])
    h = _rmsnorm(x, g_ref[...]).astype(BF16)
    out = x
    fill_gap()
    for c0, width in chunks:
        gate = _dot(h, wi_ref[:, c0:c0 + width])
        fill_gap()
        up = _dot(h, wi_ref[:, D_FF + c0:D_FF + c0 + width])
        fill_gap()
        act = (gate * jax.nn.sigmoid(gate) * up).astype(BF16)
        out = out + _dot(act, wo_ref[c0:c0 + width, :])
        fill_gap()
    assert not between
    if final_norm:
        out = _rmsnorm(out, fg_ref[...])
    o_ref[...] = out


def _ffn_kernel(a_ref, wp_ref, x_ref, g_ref, wi_ref, wo_ref, fg_ref, o_ref, *, final_norm):
    mixed = jnp.concatenate([a_ref[hd] for hd in range(a_ref.shape[0])], axis=1)
    _ffn_body(mixed, wp_ref, x_ref, g_ref, wi_ref, wo_ref, fg_ref, o_ref, final_norm)


def _ret_ffn_kernel(q_ref, kt_ref, v_ref, sg_ref, w_ref, qd_ref, kd_ref,
                    wp_ref, x_ref, g_ref, wi_ref, wo_ref, fg_ref, o_ref, gated_ref, state_ref,
                    *, tiles_per_seq, block_decay):
    i = pl.program_id(0)

    @pl.when(i == 0)
    def _():
        gated_ref[...] = jnp.zeros_like(gated_ref)

    @pl.when(i % tiles_per_seq == 0)
    def _():
        state_ref[...] = jnp.zeros_like(state_ref)

    stages = _retention_stages(q_ref, kt_ref, v_ref, sg_ref, w_ref, qd_ref, kd_ref, gated_ref, state_ref,
                               block_decay)
    _ffn_body(gated_ref[...], wp_ref, x_ref, g_ref, wi_ref, wo_ref, fg_ref, o_ref, True,
              chunks=RET_FFN_CHUNKS, between=stages)


def _ffn_specs(w_proj, layer, rows=lambda i: (i, 0)):
    tm = ROW_TILE
    return [
        pl.BlockSpec(w_proj.shape, lambda i: (0, 0), pipeline_mode=pl.Buffered(1)),
        pl.BlockSpec((tm, D_MODEL), rows),
        pl.BlockSpec((1, D_MODEL), lambda i: (0, 0)),
        pl.BlockSpec((None, D_MODEL, 2 * D_FF), lambda i: (layer, 0, 0), pipeline_mode=pl.Buffered(1)),
        pl.BlockSpec((None, D_FF, D_MODEL), lambda i: (layer, 0, 0), pipeline_mode=pl.Buffered(1)),
        pl.BlockSpec((1, D_MODEL), lambda i: (0, 0)),
    ]


def _ffn(mixed, w_proj, x2d, g, w_in_all, w_out_all, layer, final_g, final_norm, name):
    t = x2d.shape[0]
    tm = ROW_TILE
    return pl.pallas_call(
        functools.partial(_ffn_kernel, final_norm=final_norm),
        grid=(t // tm,),
        in_specs=[pl.BlockSpec((mixed.shape[0], tm, mixed.shape[2]), lambda i: (0, i, 0))]
        + _ffn_specs(w_proj, layer),
        out_specs=pl.BlockSpec((tm, D_MODEL), lambda i: (i, 0)),
        out_shape=jax.ShapeDtypeStruct((t, D_MODEL), F32),
        compiler_params=_params("arbitrary"),
        name=name,
    )(mixed, w_proj, x2d, g, w_in_all, w_out_all, final_g)


def _ret_ffn(q, kt, v, sg, w_proj, x2d, g, w_in_all, w_out_all, layer, final_g, seq):
    t = x2d.shape[0]
    tm = ROW_TILE
    r = RET_BLOCK
    n = t // tm
    mask, qd, kd, block_decay = _retention_tables()
    retained = lambda i: (jnp.minimum(i, n - 1), 0)
    finished = lambda i: (jnp.maximum(i - 1, 0), 0)
    const3 = lambda i: (0, 0, 0)
    once = dict(pipeline_mode=pl.Buffered(1))
    return pl.pallas_call(
        functools.partial(_ret_ffn_kernel, tiles_per_seq=seq // tm, block_decay=block_decay),
        grid=(n + 1,),
        in_specs=[
            pl.BlockSpec((tm, D_MODEL), retained),
            pl.BlockSpec((None, D_MODEL, tm), lambda i: (jnp.minimum(i, n - 1), 0, 0)),
            pl.BlockSpec((tm, 2 * D_MODEL), retained),
            pl.BlockSpec((tm, 2 * D_MODEL), retained),
            pl.BlockSpec((RET_HEADS, r, r), const3, **once),
            pl.BlockSpec((RET_HEADS, r, RET_QK_DIM), const3, **once),
            pl.BlockSpec((RET_HEADS, 1, r), const3, **once),
        ]
        + _ffn_specs(w_proj, layer, finished),
        out_specs=pl.BlockSpec((tm, D_MODEL), finished),
        out_shape=jax.ShapeDtypeStruct((t, D_MODEL), F32),
        scratch_shapes=[
            pltpu.VMEM((tm, 2 * D_MODEL), BF16),
            pltpu.VMEM((RET_HEADS, RET_QK_DIM, RET_V_DIM), F32),
        ],
        compiler_params=_params("arbitrary"),
        name="ret_ffn",
    )(q, kt, v, sg, mask, qd, kd, w_proj, x2d, g, w_in_all, w_out_all, final_g)


def _ret_project_head(hd, h, wq_ref, wk_ref, wv_ref, wg_ref, cos_ref, sin_ref, cost_ref, sint_ref,
                      q_ref, kt_ref, v_ref, sg_ref):
    half = RET_QK_DIM // 2
    c0 = hd * RET_QK_DIM
    cos, sin = cos_ref[...], sin_ref[...]
    acc = _dot(h, wq_ref[:, c0:c0 + RET_QK_DIM])
    t1, t2 = acc[:, :half], acc[:, half:]
    q_ref[:, c0:c0 + half] = (t1 * cos - t2 * sin).astype(BF16)
    q_ref[:, c0 + half:c0 + RET_QK_DIM] = (t1 * sin + t2 * cos).astype(BF16)
    cos_t, sin_t = cost_ref[...], sint_ref[...]
    k_scale = RET_QK_DIM ** -0.5
    acc = jnp.transpose(_dot(h, wk_ref[:, c0:c0 + RET_QK_DIM]))
    t1, t2 = acc[:half], acc[half:]
    kt_ref[c0:c0 + half, :] = ((t1 * cos_t - t2 * sin_t) * k_scale).astype(BF16)
    kt_ref[c0 + half:c0 + RET_QK_DIM, :] = ((t1 * sin_t + t2 * cos_t) * k_scale).astype(BF16)
    v0 = hd * RET_V_DIM
    gate = _dot(h, wg_ref[:, v0:v0 + RET_V_DIM])
    sg_ref[:, v0:v0 + RET_V_DIM] = (gate * jax.nn.sigmoid(gate)).astype(BF16)
    v_ref[:, v0:v0 + RET_V_DIM] = _dot(h, wv_ref[:, v0:v0 + RET_V_DIM]).astype(BF16)


def _retention_stages(q_ref, kt_ref, v_ref, sg_ref, w_ref, qd_ref, kd_ref, o_ref, state_ref, block_decay):
    r = RET_BLOCK
    stages = []
    for hd in range(RET_HEADS):
        qk_cols = slice(hd * RET_QK_DIM, (hd + 1) * RET_QK_DIM)
        v_cols = slice(hd * RET_V_DIM, (hd + 1) * RET_V_DIM)
        for blk in range(q_ref.shape[0] // r):
            rows = slice(blk * r, (blk + 1) * r)
            held = {}

            def score(hd=hd, rows=rows, qk_cols=qk_cols, held=held):
                q = q_ref[rows, qk_cols]
                kt = kt_ref[qk_cols, rows]
                held["p"] = (_dot(q, kt) * w_ref[hd]).astype(BF16)
                held["q_dec"] = (q.astype(F32) * qd_ref[hd]).astype(BF16)
                held["kt_dec"] = (kt.astype(F32) * kd_ref[hd]).astype(BF16)

            def mix(hd=hd, rows=rows, v_cols=v_cols, held=held):
                v = v_ref[rows, v_cols]
                state = state_ref[hd]
                held["y"] = _dot(jnp.concatenate([held["p"], held["q_dec"]], axis=1),
                                 jnp.concatenate([v, state.astype(BF16)], axis=0))
                state_ref[hd] = state * block_decay[hd] + _dot(held["kt_dec"], v)

            def norm(rows=rows, v_cols=v_cols, held=held):
                y = held["y"]
                mu = jnp.mean(y, axis=-1, keepdims=True)
                yc = y - mu
                var = jnp.mean(yc * yc, axis=-1, keepdims=True)
                normed = (yc * lax.rsqrt(var + EPS)).astype(BF16)
                o_ref[rows, v_cols] = sg_ref[rows, v_cols] * normed

            stages += [score, mix, norm]
    return stages


def _retention_tables():
    r = RET_BLOCK
    gamma = 1.0 - 2.0 ** (-5.0 - np.arange(RET_HEADS, dtype=np.float64))
    pos = np.arange(r)
    diff = (pos[:, None] - pos[None, :]).astype(np.float64)
    same = (pos[:, None] // CHUNK) == (pos[None, :] // CHUNK)
    earlier = (pos[None, :] // CHUNK) < (pos[:, None] // CHUNK)
    expo = np.where(same, np.abs(diff), diff)
    w = np.where((same | earlier)[None], gamma[:, None, None] ** expo[None], 0.0)
    idx = np.arange(r, dtype=np.float64)
    qd = gamma[:, None] ** (idx[None, :] + 1.0)
    qd = np.broadcast_to(qd[..., None], (RET_HEADS, r, RET_QK_DIM))
    kd = (gamma[:, None] ** (r - 1.0 - idx[None, :]))[:, None, :]
    block_decay = tuple(float(g) ** r for g in gamma)
    f32 = lambda a: np.ascontiguousarray(a, dtype=np.float32)
    return f32(w), f32(qd), f32(kd), block_decay


def _ret_proj_kernel(x_ref, g_ref, wq_ref, wk_ref, wv_ref, wg_ref, cos_ref, sin_ref, cost_ref, sint_ref,
                     q_ref, kt_ref, v_ref, sg_ref):
    h = _rmsnorm(x_ref[...], g_ref[...]).astype(BF16)
    for hd in range(RET_HEADS):
        _ret_project_head(hd, h, wq_ref, wk_ref, wv_ref, wg_ref, cos_ref, sin_ref, cost_ref, sint_ref,
                          q_ref, kt_ref, v_ref, sg_ref)


def _ret_proj(x2d, g, w, cos, sin, seq):
    t = x2d.shape[0]
    tm = ROW_TILE
    tiles_per_seq = seq // tm
    half = RET_QK_DIM // 2
    row = lambda i: (i, 0)
    const = lambda i: (0, 0)
    return pl.pallas_call(
        _ret_proj_kernel,
        grid=(t // tm,),
        in_specs=[
            pl.BlockSpec((tm, D_MODEL), row),
            pl.BlockSpec((1, D_MODEL), const),
            pl.BlockSpec((D_MODEL, D_MODEL), const, pipeline_mode=pl.Buffered(1)),
            pl.BlockSpec((D_MODEL, D_MODEL), lambda i: (0, 1), pipeline_mode=pl.Buffered(1)),
            pl.BlockSpec((D_MODEL, 2 * D_MODEL), lambda i: (0, 1), pipeline_mode=pl.Buffered(1)),
            pl.BlockSpec((D_MODEL, 2 * D_MODEL), lambda i: (0, 2), pipeline_mode=pl.Buffered(1)),
            pl.BlockSpec((tm, half), lambda i: (i % tiles_per_seq, 0)),
            pl.BlockSpec((tm, half), lambda i: (i % tiles_per_seq, 0)),
            pl.BlockSpec((half, tm), lambda i: (0, i % tiles_per_seq)),
            pl.BlockSpec((half, tm), lambda i: (0, i % tiles_per_seq)),
        ],
        out_specs=[
            pl.BlockSpec((tm, D_MODEL), row),
            pl.BlockSpec((None, D_MODEL, tm), lambda i: (i, 0, 0)),
            pl.BlockSpec((tm, 2 * D_MODEL), row),
            pl.BlockSpec((tm, 2 * D_MODEL), row),
        ],
        out_shape=[
            jax.ShapeDtypeStruct((t, D_MODEL), BF16),
            jax.ShapeDtypeStruct((t // tm, D_MODEL, tm), BF16),
            jax.ShapeDtypeStruct((t, 2 * D_MODEL), BF16),
            jax.ShapeDtypeStruct((t, 2 * D_MODEL), BF16),
        ],
        compiler_params=_params("arbitrary"),
        name="ret_proj",
    )(x2d, g, w, w, w, w, cos, sin, np.ascontiguousarray(cos.T), np.ascontiguousarray(sin.T))


def kernel(x, norm_mix, norm_ffn, fox_w_in, fox_b_f, fox_w_out, ret_w_in, ret_w_out,
           ffn_w_in, ffn_w_out, final_norm):
    batch, seq, d = x.shape
    t = batch * seq
    x2d = x.reshape(t, d)
    row = lambda a: a.reshape(1, -1)

    w_in = fox_w_in[0]
    w_qkv = w_in.astype(BF16)
    pad = LANES - C_PARTS * FOX_HEADS
    wf = jnp.pad(jnp.tile(w_in[:, 3 * D_MODEL:], (1, C_PARTS)), ((0, 0), (0, pad))).astype(BF16)
    bf = jnp.pad(jnp.tile(fox_b_f[0], C_PARTS), (0, pad)).reshape(1, LANES)
    k, qvt, ca = _fox_proj(x2d, row(norm_mix[0]), w_qkv, wf, bf, seq)
    attn = _fox_attn(k, ca, qvt, batch, seq)
    ffn_in = ffn_w_in.astype(BF16)
    ffn_out = ffn_w_out.astype(BF16)
    x2d = _ffn(attn, fox_w_out[0].astype(BF16), x2d, row(norm_ffn[0]), ffn_in, ffn_out, 0,
               row(final_norm), False, "ffn0")

    half = RET_QK_DIM // 2
    inv = ROPE_BASE ** (-np.arange(half, dtype=np.float64) / half)
    ang = np.arange(seq, dtype=np.float64)[:, None] * inv[None, :]
    cos, sin = np.cos(ang).astype(np.float32), np.sin(ang).astype(np.float32)
    w_in = ret_w_in[0]
    q, kt, v, sg = _ret_proj(x2d, row(norm_mix[1]), w_in.astype(BF16), cos, sin, seq)
    x2d = _ret_ffn(q, kt, v, sg, ret_w_out[0].astype(BF16), x2d, row(norm_ffn[1]), ffn_in, ffn_out, 1,
                   row(final_norm), seq)
    return x2d.reshape(batch, seq, d)
```

```python
import functools
import math

import jax
import jax.numpy as jnp
import numpy as np
from jax import lax
from jax.experimental import pallas as pl
from jax.experimental.pallas import tpu as pltpu

D_MODEL = 1024
CHUNK = 64
FOX_HEADS = 8
FOX_HEAD_DIM = D_MODEL // FOX_HEADS
RET_HEADS = 4
RET_QK_DIM = D_MODEL // RET_HEADS
RET_V_DIM = 2 * D_MODEL // RET_HEADS
D_FF = -(-8 * D_MODEL // (3 * 256)) * 256
ROPE_BASE = 10000.0
EPS = 1e-6
LOG2E = math.log2(math.e)

F32 = jnp.float32
BF16 = jnp.bfloat16

VMEM_LIMIT_BYTES = 56 * 1024 * 1024
LANES = 128

ROW_TILE = 512
ATTN_Q_TILE = 2 * ROW_TILE
ATTN_K_TILE = ATTN_Q_TILE // 2
RET_BLOCK = 256
FFN_CHUNKS = ((0, 1024), (1024, 1024), (2048, 768))
RET_FFN_CHUNKS = tuple((c0, min(512, D_FF - c0)) for c0 in range(0, D_FF, 512))
LEAD_STAGES = 2
C_PARTS = 3
ONES_ROWS = 16


def _params(*semantics):
    return pltpu.CompilerParams(dimension_semantics=semantics, vmem_limit_bytes=VMEM_LIMIT_BYTES)


def _rmsnorm(x, g):
    return x * lax.rsqrt(jnp.mean(x * x, axis=-1, keepdims=True) + EPS) * g


def _dot(a, b):
    return jnp.dot(a, b, preferred_element_type=F32)


def _cumsum_rows(v):
    n = v.shape[0]
    row = lax.broadcasted_iota(jnp.int32, v.shape, 0)
    shift = 1
    while shift < n:
        rolled = pltpu.roll(v, shift, axis=0)
        v = v + jnp.where(row >= shift, rolled, 0.0)
        shift *= 2
    return v


def _fox_proj_kernel(x_ref, g_ref, wq_ref, wk_ref, wv_ref, wf_ref, bf_ref, k_ref, qvt_ref, ca_ref, carry_ref,
                     *, tiles_per_seq):
    i = pl.program_id(0)
    tm = x_ref.shape[0]
    h = _rmsnorm(x_ref[...], g_ref[...]).astype(BF16)
    logits = _dot(h, wf_ref[...]) + bf_ref[...]
    log_f = jnp.minimum(logits, 0.0) - jnp.log1p(jnp.exp(-jnp.abs(logits)))
    carry = jnp.where(i % tiles_per_seq == 0, 0.0, carry_ref[...])
    c = _cumsum_rows(log_f) + carry
    carry_ref[...] = c[tm - 1:tm, :]
    neg_c = c * (-LOG2E)
    hi = neg_c.astype(BF16)
    rest = neg_c - hi.astype(F32)
    mid = rest.astype(BF16)
    lo = (rest - mid.astype(F32)).astype(BF16)
    lane = lax.broadcasted_iota(jnp.int32, neg_c.shape, 1)
    ca_ref[...] = jnp.where(lane < FOX_HEADS, hi, jnp.where(lane < 2 * FOX_HEADS, mid, lo))

    q_scale = FOX_HEAD_DIM ** -0.5 * LOG2E
    tn = 512
    for n0 in range(0, D_MODEL, tn):
        acc = _dot(h, wk_ref[:, n0:n0 + tn]).astype(BF16)
        for d0 in range(0, tn, FOX_HEAD_DIM):
            k_ref[(n0 + d0) // FOX_HEAD_DIM] = acc[:, d0:d0 + FOX_HEAD_DIM]
    for n0 in range(0, D_MODEL, tn):
        acc = jnp.transpose(_dot(h, wq_ref[:, n0:n0 + tn])) * q_scale
        qvt_ref[n0:n0 + tn, :] = acc.astype(BF16)
    for n0 in range(0, D_MODEL, tn):
        acc = jnp.transpose(_dot(h, wv_ref[:, n0:n0 + tn]))
        qvt_ref[D_MODEL + n0:D_MODEL + n0 + tn, :] = acc.astype(BF16)


def _fox_proj(x2d, g, w_qkv, wf, bf, seq):
    t = x2d.shape[0]
    tm = ROW_TILE
    window = lambda col: pl.BlockSpec((D_MODEL, D_MODEL), lambda i: (0, col), pipeline_mode=pl.Buffered(1))
    return pl.pallas_call(
        functools.partial(_fox_proj_kernel, tiles_per_seq=seq // tm),
        grid=(t // tm,),
        in_specs=[
            pl.BlockSpec((tm, D_MODEL), lambda i: (i, 0)),
            pl.BlockSpec((1, D_MODEL), lambda i: (0, 0)),
            window(0),
            window(1),
            window(2),
            pl.BlockSpec((D_MODEL, LANES), lambda i: (0, 0)),
            pl.BlockSpec((1, LANES), lambda i: (0, 0)),
        ],
        out_specs=[
            pl.BlockSpec((FOX_HEADS, tm, FOX_HEAD_DIM), lambda i: (0, i, 0)),
            pl.BlockSpec((None, 2 * D_MODEL, tm), lambda i: (i, 0, 0)),
            pl.BlockSpec((tm, LANES), lambda i: (i, 0)),
        ],
        out_shape=[
            jax.ShapeDtypeStruct((FOX_HEADS, t, FOX_HEAD_DIM), BF16),
            jax.ShapeDtypeStruct((t // tm, 2 * D_MODEL, tm), BF16),
            jax.ShapeDtypeStruct((t, LANES), BF16),
        ],
        scratch_shapes=[pltpu.VMEM((1, LANES), F32)],
        compiler_params=_params("arbitrary"),
        name="fox_proj",
    )(x2d, g, w_qkv, w_qkv, w_qkv, wf, bf)


def _fox_attn_kernel(qt_ref, k_ref, ca_ref, vt_ref, o_ref, s_ref, smax_ref, m_ref, acc_ref, done_ref):
    head = pl.program_id(1)
    tq = ATTN_Q_TILE
    tk = ATTN_K_TILE
    tv = vt_ref.shape[2]
    nq = (qt_ref.shape[0] * tv) // tq
    assert tq == 2 * tk and tk == tv

    piece = lax.broadcasted_iota(jnp.int32, (LANES, tq), 0)
    pick = jnp.logical_and(piece % FOX_HEADS == head, piece < C_PARTS * FOX_HEADS)
    ones_rows = jnp.where(pick, 1.0, 0.0).astype(BF16)

    def queries(i):
        return jnp.concatenate([jnp.concatenate([qt_ref[2 * i], qt_ref[2 * i + 1]], axis=1), ones_rows], axis=0)

    def scores(qt_ext, row0, c0=0):
        k_ext = jnp.concatenate([k_ref[pl.ds(row0, tk), :], ca_ref[pl.ds(row0, tk), :]], axis=1)
        return _dot(k_ext, qt_ext[:, c0:])

    def update(s, vt, c0=0, diagonal=False, tile_max=None):
        if diagonal:
            key = lax.broadcasted_iota(jnp.int32, s.shape, 0)
            qry = lax.broadcasted_iota(jnp.int32, s.shape, 1)
            s = jnp.where(qry >= key, s, -jnp.inf)
        if tile_max is None:
            tile_max = jnp.max(s, axis=0, keepdims=True)
        m_old = m_ref[:, c0:]
        m_new = jnp.maximum(m_old, tile_max)
        alpha = jnp.exp2(m_old - m_new)
        p = jnp.exp2(s - m_new).astype(BF16)
        vt_ext = jnp.concatenate([vt, jnp.ones((ONES_ROWS, vt.shape[1]), BF16)], axis=0)
        acc_ref[:, c0:] = alpha * acc_ref[:, c0:] + _dot(vt_ext, p)
        m_ref[:, c0:] = m_new

    def stash(slot, qt_ext, row0):
        s = scores(qt_ext, row0)
        s_ref[slot] = s
        smax_ref[slot] = jnp.max(s, axis=0, keepdims=True)

    def query_tile(i, head_slot, next_slot, odd):
        qt_ext = queries(i)
        m_ref[...] = jnp.full_like(m_ref, -jnp.inf)
        acc_ref[...] = jnp.zeros_like(acc_ref)

        def body(j, carry):
            base = pl.multiple_of(j * tq, tq)
            stash(1, qt_ext, base + tk)
            update(s_ref[head_slot], vt_ref[2 * j], tile_max=smax_ref[head_slot])
            stash(head_slot, qt_ext, base + tq)
            update(s_ref[1], vt_ref[2 * j + 1], tile_max=smax_ref[1])
            return carry

        def two_steps(jj, carry):
            body(2 * jj, carry)
            return body(2 * jj + 1, carry)

        lax.fori_loop(0, i // 2, two_steps, 0)
        if odd:
            body(i - 1, 0)

        base = pl.multiple_of(i * tq, tq)
        s_last = scores(qt_ext, base + tk, tk)
        qt_next = queries(jnp.minimum(i + 1, nq - 1))
        stash(next_slot, qt_next, 0)
        finish(pl.multiple_of(jnp.maximum(i - 1, 0) * tq, tq))
        update(s_ref[head_slot], vt_ref[2 * i], 0, diagonal=True)
        update(s_last, vt_ref[2 * i + 1], tk, diagonal=True)
        done_ref[...] = acc_ref[...]

    def two_query_tiles(pair, carry):
        query_tile(2 * pair, 0, 2, odd=False)
        query_tile(2 * pair + 1, 2, 0, odd=True)
        return carry

    def finish(row0):
        done = done_ref[...]
        out = done[:FOX_HEAD_DIM] / done[FOX_HEAD_DIM:FOX_HEAD_DIM + 1]
        o_ref[pl.ds(row0, tq), :] = jnp.transpose(out).astype(o_ref.dtype)

    done_ref[...] = jnp.ones_like(done_ref)
    stash(0, queries(0), 0)
    assert nq % 2 == 0
    lax.fori_loop(0, nq // 2, two_query_tiles, 0)
    finish((nq - 1) * tq)


def _fox_attn(k, ca, qvt, batch, seq):
    t = k.shape[1]
    tq = ATTN_Q_TILE
    tk = ATTN_K_TILE
    tv = qvt.shape[2]
    hb = D_MODEL // FOX_HEAD_DIM
    return pl.pallas_call(
        _fox_attn_kernel,
        grid=(batch, FOX_HEADS),
        in_specs=[
            pl.BlockSpec((seq // tv, FOX_HEAD_DIM, tv), lambda b, h: (b, h, 0)),
            pl.BlockSpec((None, seq, FOX_HEAD_DIM), lambda b, h: (h, b, 0)),
            pl.BlockSpec((seq, LANES), lambda b, h: (b, 0)),
            pl.BlockSpec((seq // tv, FOX_HEAD_DIM, tv), lambda b, h: (b, hb + h, 0)),
        ],
        out_specs=pl.BlockSpec((None, seq, FOX_HEAD_DIM), lambda b, h: (h, b, 0)),
        out_shape=jax.ShapeDtypeStruct((FOX_HEADS, t, FOX_HEAD_DIM), BF16),
        scratch_shapes=[
            pltpu.VMEM((3, tk, tq), F32),
            pltpu.VMEM((3, 1, tq), F32),
            pltpu.VMEM((1, tq), F32),
            pltpu.VMEM((FOX_HEAD_DIM + ONES_ROWS, tq), F32),
            pltpu.VMEM((FOX_HEAD_DIM + ONES_ROWS, tq), F32),
        ],
        compiler_params=_params("arbitrary", "arbitrary"),
        name="fox_attn",
    )(qvt, k, ca, qvt)


def _ffn_body(mixed, wp_ref, x_ref, g_ref, wi_ref, wo_ref, fg_ref, o_ref, final_norm,
              chunks=FFN_CHUNKS, between=()):
    between = list(between)
    gaps = [2 + 3 * len(chunks)]

    def fill_gap():
        for _ in range(-(-len(between) // gaps[0])):
            between.pop(0)()
        gaps[0] -= 1

    x = x_ref[...] + _dot(mixed, wp_ref[...])
    for _ in range(min(LEAD_STAGES, len(between))):
        between.pop(0)()
    gaps[0] -= 1
    h = _rmsnorm(x, g_ref[...]).astype(BF16)
    out = x
    fill_gap()
    for c0, width in chunks:
        gate = _dot(h, wi_ref[:, c0:c0 + width])
        fill_gap()
        up = _dot(h, wi_ref[:, D_FF + c0:D_FF + c0 + width])
        fill_gap()
        act = (gate * jax.nn.sigmoid(gate) * up).astype(BF16)
        out = out + _dot(act, wo_ref[c0:c0 + width, :])
        fill_gap()
    assert not between
    if final_norm:
        out = _rmsnorm(out, fg_ref[...])
    o_ref[...] = out


def _ffn_kernel(a_ref, wp_ref, x_ref, g_ref, wi_ref, wo_ref, fg_ref, o_ref, *, final_norm):
    mixed = jnp.concatenate([a_ref[hd] for hd in range(a_ref.shape[0])], axis=1)
    _ffn_body(mixed, wp_ref, x_ref, g_ref, wi_ref, wo_ref, fg_ref, o_ref, final_norm)


def _ret_ffn_kernel(q_ref, kt_ref, v_ref, sg_ref, w_ref, qd_ref, kd_ref,
                    wp_ref, x_ref, g_ref, wi_ref, wo_ref, fg_ref, o_ref, gated_ref, state_ref,
                    *, tiles_per_seq, block_decay):
    i = pl.program_id(0)

    @pl.when(i == 0)
    def _():
        gated_ref[...] = jnp.zeros_like(gated_ref)

    @pl.when(i % tiles_per_seq == 0)
    def _():
        state_ref[...] = jnp.zeros_like(state_ref)

    stages = _retention_stages(q_ref, kt_ref, v_ref, sg_ref, w_ref, qd_ref, kd_ref, gated_ref, state_ref,
                               block_decay)
    _ffn_body(gated_ref[...], wp_ref, x_ref, g_ref, wi_ref, wo_ref, fg_ref, o_ref, True,
              chunks=RET_FFN_CHUNKS, between=stages)


def _ffn_specs(w_proj, layer, rows=lambda i: (i, 0)):
    tm = ROW_TILE
    return [
        pl.BlockSpec(w_proj.shape, lambda i: (0, 0), pipeline_mode=pl.Buffered(1)),
        pl.BlockSpec((tm, D_MODEL), rows),
        pl.BlockSpec((1, D_MODEL), lambda i: (0, 0)),
        pl.BlockSpec((None, D_MODEL, 2 * D_FF), lambda i: (layer, 0, 0), pipeline_mode=pl.Buffered(1)),
        pl.BlockSpec((None, D_FF, D_MODEL), lambda i: (layer, 0, 0), pipeline_mode=pl.Buffered(1)),
        pl.BlockSpec((1, D_MODEL), lambda i: (0, 0)),
    ]


def _ffn(mixed, w_proj, x2d, g, w_in_all, w_out_all, layer, final_g, final_norm, name):
    t = x2d.shape[0]
    tm = ROW_TILE
    return pl.pallas_call(
        functools.partial(_ffn_kernel, final_norm=final_norm),
        grid=(t // tm,),
        in_specs=[pl.BlockSpec((mixed.shape[0], tm, mixed.shape[2]), lambda i: (0, i, 0))]
        + _ffn_specs(w_proj, layer),
        out_specs=pl.BlockSpec((tm, D_MODEL), lambda i: (i, 0)),
        out_shape=jax.ShapeDtypeStruct((t, D_MODEL), F32),
        compiler_params=_params("arbitrary"),
        name=name,
    )(mixed, w_proj, x2d, g, w_in_all, w_out_all, final_g)


def _ret_ffn(q, kt, v, sg, w_proj, x2d, g, w_in_all, w_out_all, layer, final_g, seq):
    t = x2d.shape[0]
    tm = ROW_TILE
    r = RET_BLOCK
    n = t // tm
    mask, qd, kd, block_decay = _retention_tables()
    retained = lambda i: (jnp.minimum(i, n - 1), 0)
    finished = lambda i: (jnp.maximum(i - 1, 0), 0)
    const3 = lambda i: (0, 0, 0)
    once = dict(pipeline_mode=pl.Buffered(1))
    return pl.pallas_call(
        functools.partial(_ret_ffn_kernel, tiles_per_seq=seq // tm, block_decay=block_decay),
        grid=(n + 1,),
        in_specs=[
            pl.BlockSpec((tm, D_MODEL), retained),
            pl.BlockSpec((None, D_MODEL, tm), lambda i: (jnp.minimum(i, n - 1), 0, 0)),
            pl.BlockSpec((tm, 2 * D_MODEL), retained),
            pl.BlockSpec((tm, 2 * D_MODEL), retained),
            pl.BlockSpec((RET_HEADS, r, r), const3, **once),
            pl.BlockSpec((RET_HEADS, r, RET_QK_DIM), const3, **once),
            pl.BlockSpec((RET_HEADS, 1, r), const3, **once),
        ]
        + _ffn_specs(w_proj, layer, finished),
        out_specs=pl.BlockSpec((tm, D_MODEL), finished),
        out_shape=jax.ShapeDtypeStruct((t, D_MODEL), F32),
        scratch_shapes=[
            pltpu.VMEM((tm, 2 * D_MODEL), BF16),
            pltpu.VMEM((RET_HEADS, RET_QK_DIM, RET_V_DIM), F32),
        ],
        compiler_params=_params("arbitrary"),
        name="ret_ffn",
    )(q, kt, v, sg, mask, qd, kd, w_proj, x2d, g, w_in_all, w_out_all, final_g)


def _ret_project_head(hd, h, wq_ref, wk_ref, wv_ref, wg_ref, cos_ref, sin_ref, cost_ref, sint_ref,
                      q_ref, kt_ref, v_ref, sg_ref):
    half = RET_QK_DIM // 2
    c0 = hd * RET_QK_DIM
    cos, sin = cos_ref[...], sin_ref[...]
    acc = _dot(h, wq_ref[:, c0:c0 + RET_QK_DIM])
    t1, t2 = acc[:, :half], acc[:, half:]
    q_ref[:, c0:c0 + half] = (t1 * cos - t2 * sin).astype(BF16)
    q_ref[:, c0 + half:c0 + RET_QK_DIM] = (t1 * sin + t2 * cos).astype(BF16)
    cos_t, sin_t = cost_ref[...], sint_ref[...]
    k_scale = RET_QK_DIM ** -0.5
    acc = jnp.transpose(_dot(h, wk_ref[:, c0:c0 + RET_QK_DIM]))
    t1, t2 = acc[:half], acc[half:]
    kt_ref[c0:c0 + half, :] = ((t1 * cos_t - t2 * sin_t) * k_scale).astype(BF16)
    kt_ref[c0 + half:c0 + RET_QK_DIM, :] = ((t1 * sin_t + t2 * cos_t) * k_scale).astype(BF16)
    v0 = hd * RET_V_DIM
    gate = _dot(h, wg_ref[:, v0:v0 + RET_V_DIM])
    sg_ref[:, v0:v0 + RET_V_DIM] = (gate * jax.nn.sigmoid(gate)).astype(BF16)
    v_ref[:, v0:v0 + RET_V_DIM] = _dot(h, wv_ref[:, v0:v0 + RET_V_DIM]).astype(BF16)


def _retention_stages(q_ref, kt_ref, v_ref, sg_ref, w_ref, qd_ref, kd_ref, o_ref, state_ref, block_decay):
    r = RET_BLOCK
    stages = []
    for hd in range(RET_HEADS):
        qk_cols = slice(hd * RET_QK_DIM, (hd + 1) * RET_QK_DIM)
        v_cols = slice(hd * RET_V_DIM, (hd + 1) * RET_V_DIM)
        for blk in range(q_ref.shape[0] // r):
            rows = slice(blk * r, (blk + 1) * r)
            held = {}

            def score(hd=hd, rows=rows, qk_cols=qk_cols, held=held):
                q = q_ref[rows, qk_cols]
                kt = kt_ref[qk_cols, rows]
                held["p"] = (_dot(q, kt) * w_ref[hd]).astype(BF16)
                held["q_dec"] = (q.astype(F32) * qd_ref[hd]).astype(BF16)
                held["kt_dec"] = (kt.astype(F32) * kd_ref[hd]).astype(BF16)

            def mix(hd=hd, rows=rows, v_cols=v_cols, held=held):
                v = v_ref[rows, v_cols]
                state = state_ref[hd]
                held["y"] = _dot(jnp.concatenate([held["p"], held["q_dec"]], axis=1),
                                 jnp.concatenate([v, state.astype(BF16)], axis=0))
                state_ref[hd] = state * block_decay[hd] + _dot(held["kt_dec"], v)

            def norm(rows=rows, v_cols=v_cols, held=held):
                y = held["y"]
                mu = jnp.mean(y, axis=-1, keepdims=True)
                yc = y - mu
                var = jnp.mean(yc * yc, axis=-1, keepdims=True)
                normed = (yc * lax.rsqrt(var + EPS)).astype(BF16)
                o_ref[rows, v_cols] = sg_ref[rows, v_cols] * normed

            stages += [score, mix, norm]
    return stages


def _retention_tables():
    r = RET_BLOCK
    gamma = 1.0 - 2.0 ** (-5.0 - np.arange(RET_HEADS, dtype=np.float64))
    pos = np.arange(r)
    diff = (pos[:, None] - pos[None, :]).astype(np.float64)
    same = (pos[:, None] // CHUNK) == (pos[None, :] // CHUNK)
    earlier = (pos[None, :] // CHUNK) < (pos[:, None] // CHUNK)
    expo = np.where(same, np.abs(diff), diff)
    w = np.where((same | earlier)[None], gamma[:, None, None] ** expo[None], 0.0)
    idx = np.arange(r, dtype=np.float64)
    qd = gamma[:, None] ** (idx[None, :] + 1.0)
    qd = np.broadcast_to(qd[..., None], (RET_HEADS, r, RET_QK_DIM))
    kd = (gamma[:, None] ** (r - 1.0 - idx[None, :]))[:, None, :]
    block_decay = tuple(float(g) ** r for g in gamma)
    f32 = lambda a: np.ascontiguousarray(a, dtype=np.float32)
    return f32(w), f32(qd), f32(kd), block_decay


def _ret_proj_kernel(x_ref, g_ref, wq_ref, wk_ref, wv_ref, wg_ref, cos_ref, sin_ref, cost_ref, sint_ref,
                     q_ref, kt_ref, v_ref, sg_ref):
    h = _rmsnorm(x_ref[...], g_ref[...]).astype(BF16)
    for hd in range(RET_HEADS):
        _ret_project_head(hd, h, wq_ref, wk_ref, wv_ref, wg_ref, cos_ref, sin_ref, cost_ref, sint_ref,
                          q_ref, kt_ref, v_ref, sg_ref)


def _ret_proj(x2d, g, w, cos, sin, seq):
    t = x2d.shape[0]
    tm = ROW_TILE
    tiles_per_seq = seq // tm
    half = RET_QK_DIM // 2
    row = lambda i: (i, 0)
    const = lambda i: (0, 0)
    return pl.pallas_call(
        _ret_proj_kernel,
        grid=(t // tm,),
        in_specs=[
            pl.BlockSpec((tm, D_MODEL), row),
            pl.BlockSpec((1, D_MODEL), const),
            pl.BlockSpec((D_MODEL, D_MODEL), const, pipeline_mode=pl.Buffered(1)),
            pl.BlockSpec((D_MODEL, D_MODEL), lambda i: (0, 1), pipeline_mode=pl.Buffered(1)),
            pl.BlockSpec((D_MODEL, 2 * D_MODEL), lambda i: (0, 1), pipeline_mode=pl.Buffered(1)),
            pl.BlockSpec((D_MODEL, 2 * D_MODEL), lambda i: (0, 2), pipeline_mode=pl.Buffered(1)),
            pl.BlockSpec((tm, half), lambda i: (i % tiles_per_seq, 0)),
            pl.BlockSpec((tm, half), lambda i: (i % tiles_per_seq, 0)),
            pl.BlockSpec((half, tm), lambda i: (0, i % tiles_per_seq)),
            pl.BlockSpec((half, tm), lambda i: (0, i % tiles_per_seq)),
        ],
        out_specs=[
            pl.BlockSpec((tm, D_MODEL), row),
            pl.BlockSpec((None, D_MODEL, tm), lambda i: (i, 0, 0)),
            pl.BlockSpec((tm, 2 * D_MODEL), row),
            pl.BlockSpec((tm, 2 * D_MODEL), row),
        ],
        out_shape=[
            jax.ShapeDtypeStruct((t, D_MODEL), BF16),
            jax.ShapeDtypeStruct((t // tm, D_MODEL, tm), BF16),
            jax.ShapeDtypeStruct((t, 2 * D_MODEL), BF16),
            jax.ShapeDtypeStruct((t, 2 * D_MODEL), BF16),
        ],
        compiler_params=_params("arbitrary"),
        name="ret_proj",
    )(x2d, g, w, w, w, w, cos, sin, np.ascontiguousarray(cos.T), np.ascontiguousarray(sin.T))


def kernel(x, norm_mix, norm_ffn, fox_w_in, fox_b_f, fox_w_out, ret_w_in, ret_w_out,
           ffn_w_in, ffn_w_out, final_norm):
    batch, seq, d = x.shape
    t = batch * seq
    x2d = x.reshape(t, d)
    row = lambda a: a.reshape(1, -1)

    w_in = fox_w_in[0]
    w_qkv = w_in.astype(BF16)
    pad = LANES - C_PARTS * FOX_HEADS
    wf = jnp.pad(jnp.tile(w_in[:, 3 * D_MODEL:], (1, C_PARTS)), ((0, 0), (0, pad))).astype(BF16)
    bf = jnp.pad(jnp.tile(fox_b_f[0], C_PARTS), (0, pad)).reshape(1, LANES)
    k, qvt, ca = _fox_proj(x2d, row(norm_mix[0]), w_qkv, wf, bf, seq)
    attn = _fox_attn(k, ca, qvt, batch, seq)
    ffn_in = ffn_w_in.astype(BF16)
    ffn_out = ffn_w_out.astype(BF16)
    x2d = _ffn(attn, fox_w_out[0].astype(BF16), x2d, row(norm_ffn[0]), ffn_in, ffn_out, 0,
               row(final_norm), False, "ffn0")

    half = RET_QK_DIM // 2
    inv = ROPE_BASE ** (-np.arange(half, dtype=np.float64) / half)
    ang = np.arange(seq, dtype=np.float64)[:, None] * inv[None, :]
    cos, sin = np.cos(ang).astype(np.float32), np.sin(ang).astype(np.float32)
    w_in = ret_w_in[0]
    q, kt, v, sg = _ret_proj(x2d, row(norm_mix[1]), w_in.astype(BF16), cos, sin, seq)
    x2d = _ret_ffn(q, kt, v, sg, ret_w_out[0].astype(BF16), x2d, row(norm_ffn[1]), ffn_in, ffn_out, 1,
                   row(final_norm), seq)
    return x2d.reshape(batch, seq, d)
```

```python
import functools
import math

import jax
import jax.numpy as jnp
import numpy as np
from jax import lax
from jax.experimental import pallas as pl
from jax.experimental.pallas import tpu as pltpu

D_MODEL = 1024
CHUNK = 64
FOX_HEADS = 8
FOX_HEAD_DIM = D_MODEL // FOX_HEADS
RET_HEADS = 4
RET_QK_DIM = D_MODEL // RET_HEADS
RET_V_DIM = 2 * D_MODEL // RET_HEADS
D_FF = -(-8 * D_MODEL // (3 * 256)) * 256
ROPE_BASE = 10000.0
EPS = 1e-6
LOG2E = math.log2(math.e)

F32 = jnp.float32
BF16 = jnp.bfloat16

VMEM_LIMIT_BYTES = 56 * 1024 * 1024
LANES = 128

ROW_TILE = 512
ATTN_Q_TILE = 2 * ROW_TILE
ATTN_K_TILE = ATTN_Q_TILE // 2
RET_BLOCK = 256
FFN_CHUNKS = ((0, 1024), (1024, 1024), (2048, 768))
RET_FFN_CHUNKS = tuple((c0, min(512, D_FF - c0)) for c0 in range(0, D_FF, 512))
LEAD_STAGES = 2
C_PARTS = 3
ONES_ROWS = 16


def _params(*semantics):
    return pltpu.CompilerParams(dimension_semantics=semantics, vmem_limit_bytes=VMEM_LIMIT_BYTES)


def _rmsnorm(x, g):
    return x * lax.rsqrt(jnp.mean(x * x, axis=-1, keepdims=True) + EPS) * g


def _dot(a, b):
    return jnp.dot(a, b, preferred_element_type=F32)


def _cumsum_rows(v):
    n = v.shape[0]
    row = lax.broadcasted_iota(jnp.int32, v.shape, 0)
    shift = 1
    while shift < n:
        rolled = pltpu.roll(v, shift, axis=0)
        v = v + jnp.where(row >= shift, rolled, 0.0)
        shift *= 2
    return v


def _fox_proj_kernel(x_ref, g_ref, wq_ref, wk_ref, wv_ref, wf_ref, bf_ref, k_ref, qvt_ref, ca_ref, carry_ref,
                     *, tiles_per_seq):
    i = pl.program_id(0)
    tm = x_ref.shape[0]
    h = _rmsnorm(x_ref[...], g_ref[...]).astype(BF16)
    logits = _dot(h, wf_ref[...]) + bf_ref[...]
    log_f = jnp.minimum(logits, 0.0) - jnp.log1p(jnp.exp(-jnp.abs(logits)))
    carry = jnp.where(i % tiles_per_seq == 0, 0.0, carry_ref[...])
    c = _cumsum_rows(log_f) + carry
    carry_ref[...] = c[tm - 1:tm, :]
    neg_c = c * (-LOG2E)
    hi = neg_c.astype(BF16)
    rest = neg_c - hi.astype(F32)
    mid = rest.astype(BF16)
    lo = (rest - mid.astype(F32)).astype(BF16)
    lane = lax.broadcasted_iota(jnp.int32, neg_c.shape, 1)
    ca_ref[...] = jnp.where(lane < FOX_HEADS, hi, jnp.where(lane < 2 * FOX_HEADS, mid, lo))

    q_scale = FOX_HEAD_DIM ** -0.5 * LOG2E
    tn = 512
    for n0 in range(0, D_MODEL, tn):
        acc = _dot(h, wk_ref[:, n0:n0 + tn]).astype(BF16)
        for d0 in range(0, tn, FOX_HEAD_DIM):
            k_ref[(n0 + d0) // FOX_HEAD_DIM] = acc[:, d0:d0 + FOX_HEAD_DIM]
    for n0 in range(0, D_MODEL, tn):
        acc = jnp.transpose(_dot(h, wq_ref[:, n0:n0 + tn])) * q_scale
        qvt_ref[n0:n0 + tn, :] = acc.astype(BF16)
    for n0 in range(0, D_MODEL, tn):
        acc = jnp.transpose(_dot(h, wv_ref[:, n0:n0 + tn]))
        qvt_ref[D_MODEL + n0:D_MODEL + n0 + tn, :] = acc.astype(BF16)


def _fox_proj(x2d, g, w_qkv, wf, bf, seq):
    t = x2d.shape[0]
    tm = ROW_TILE
    window = lambda col: pl.BlockSpec((D_MODEL, D_MODEL), lambda i: (0, col), pipeline_mode=pl.Buffered(1))
    return pl.pallas_call(
        functools.partial(_fox_proj_kernel, tiles_per_seq=seq // tm),
        grid=(t // tm,),
        in_specs=[
            pl.BlockSpec((tm, D_MODEL), lambda i: (i, 0)),
            pl.BlockSpec((1, D_MODEL), lambda i: (0, 0)),
            window(0),
            window(1),
            window(2),
            pl.BlockSpec((D_MODEL, LANES), lambda i: (0, 0)),
            pl.BlockSpec((1, LANES), lambda i: (0, 0)),
        ],
        out_specs=[
            pl.BlockSpec((FOX_HEADS, tm, FOX_HEAD_DIM), lambda i: (0, i, 0)),
            pl.BlockSpec((None, 2 * D_MODEL, tm), lambda i: (i, 0, 0)),
            pl.BlockSpec((tm, LANES), lambda i: (i, 0)),
        ],
        out_shape=[
            jax.ShapeDtypeStruct((FOX_HEADS, t, FOX_HEAD_DIM), BF16),
            jax.ShapeDtypeStruct((t // tm, 2 * D_MODEL, tm), BF16),
            jax.ShapeDtypeStruct((t, LANES), BF16),
        ],
        scratch_shapes=[pltpu.VMEM((1, LANES), F32)],
        compiler_params=_params("arbitrary"),
        name="fox_proj",
    )(x2d, g, w_qkv, w_qkv, w_qkv, wf, bf)


def _fox_attn_kernel(qt_ref, k_ref, ca_ref, vt_ref, o_ref, s_ref, smax_ref, m_ref, acc_ref, done_ref):
    head = pl.program_id(1)
    tq = ATTN_Q_TILE
    tk = ATTN_K_TILE
    tv = vt_ref.shape[2]
    nq = (qt_ref.shape[0] * tv) // tq
    assert tq == 2 * tk and tk == tv

    piece = lax.broadcasted_iota(jnp.int32, (LANES, tq), 0)
    pick = jnp.logical_and(piece % FOX_HEADS == head, piece < C_PARTS * FOX_HEADS)
    ones_rows = jnp.where(pick, 1.0, 0.0).astype(BF16)

    def queries(i):
        return jnp.concatenate([jnp.concatenate([qt_ref[2 * i], qt_ref[2 * i + 1]], axis=1), ones_rows], axis=0)

    def scores(qt_ext, row0, c0=0):
        k_ext = jnp.concatenate([k_ref[pl.ds(row0, tk), :], ca_ref[pl.ds(row0, tk), :]], axis=1)
        return _dot(k_ext, qt_ext[:, c0:])

    def update(s, vt, c0=0, diagonal=False, tile_max=None):
        if diagonal:
            key = lax.broadcasted_iota(jnp.int32, s.shape, 0)
            qry = lax.broadcasted_iota(jnp.int32, s.shape, 1)
            s = jnp.where(qry >= key, s, -jnp.inf)
        if tile_max is None:
            tile_max = jnp.max(s, axis=0, keepdims=True)
        m_old = m_ref[:, c0:]
        m_new = jnp.maximum(m_old, tile_max)
        alpha = jnp.exp2(m_old - m_new)
        p = jnp.exp2(s - m_new).astype(BF16)
        vt_ext = jnp.concatenate([vt, jnp.ones((ONES_ROWS, vt.shape[1]), BF16)], axis=0)
        acc_ref[:, c0:] = alpha * acc_ref[:, c0:] + _dot(vt_ext, p)
        m_ref[:, c0:] = m_new

    def stash(slot, qt_ext, row0):
        s = scores(qt_ext, row0)
        s_ref[slot] = s
        smax_ref[slot] = jnp.max(s, axis=0, keepdims=True)

    def query_tile(i, head_slot, next_slot, odd):
        qt_ext = queries(i)
        m_ref[...] = jnp.full_like(m_ref, -jnp.inf)
        acc_ref[...] = jnp.zeros_like(acc_ref)

        def body(j, carry):
            base = pl.multiple_of(j * tq, tq)
            stash(1, qt_ext, base + tk)
            update(s_ref[head_slot], vt_ref[2 * j], tile_max=smax_ref[head_slot])
            stash(head_slot, qt_ext, base + tq)
            update(s_ref[1], vt_ref[2 * j + 1], tile_max=smax_ref[1])
            return carry

        def two_steps(jj, carry):
            body(2 * jj, carry)
            return body(2 * jj + 1, carry)

        lax.fori_loop(0, i // 2, two_steps, 0)
        if odd:
            body(i - 1, 0)

        base = pl.multiple_of(i * tq, tq)
        s_last = scores(qt_ext, base + tk, tk)
        qt_next = queries(jnp.minimum(i + 1, nq - 1))
        stash(next_slot, qt_next, 0)
        finish(pl.multiple_of(jnp.maximum(i - 1, 0) * tq, tq))
        update(s_ref[head_slot], vt_ref[2 * i], 0, diagonal=True)
        update(s_last, vt_ref[2 * i + 1], tk, diagonal=True)
        done_ref[...] = acc_ref[...]

    def two_query_tiles(pair, carry):
        query_tile(2 * pair, 0, 2, odd=False)
        query_tile(2 * pair + 1, 2, 0, odd=True)
        return carry

    def finish(row0):
        done = done_ref[...]
        out = done[:FOX_HEAD_DIM] / done[FOX_HEAD_DIM:FOX_HEAD_DIM + 1]
        o_ref[pl.ds(row0, tq), :] = jnp.transpose(out).astype(o_ref.dtype)

    done_ref[...] = jnp.ones_like(done_ref)
    stash(0, queries(0), 0)
    assert nq % 2 == 0
    lax.fori_loop(0, nq // 2, two_query_tiles, 0)
    finish((nq - 1) * tq)


def _fox_attn(k, ca, qvt, batch, seq):
    t = k.shape[1]
    tq = ATTN_Q_TILE
    tk = ATTN_K_TILE
    tv = qvt.shape[2]
    hb = D_MODEL // FOX_HEAD_DIM
    return pl.pallas_call(
        _fox_attn_kernel,
        grid=(batch, FOX_HEADS),
        in_specs=[
            pl.BlockSpec((seq // tv, FOX_HEAD_DIM, tv), lambda b, h: (b, h, 0)),
            pl.BlockSpec((None, seq, FOX_HEAD_DIM), lambda b, h: (h, b, 0)),
            pl.BlockSpec((seq, LANES), lambda b, h: (b, 0)),
            pl.BlockSpec((seq // tv, FOX_HEAD_DIM, tv), lambda b, h: (b, hb + h, 0)),
        ],
        out_specs=pl.BlockSpec((None, seq, FOX_HEAD_DIM), lambda b, h: (h, b, 0)),
        out_shape=jax.ShapeDtypeStruct((FOX_HEADS, t, FOX_HEAD_DIM), BF16),
        scratch_shapes=[
            pltpu.VMEM((3, tk, tq), F32),
            pltpu.VMEM((3, 1, tq), F32),
            pltpu.VMEM((1, tq), F32),
            pltpu.VMEM((FOX_HEAD_DIM + ONES_ROWS, tq), F32),
            pltpu.VMEM((FOX_HEAD_DIM + ONES_ROWS, tq), F32),
        ],
        compiler_params=_params("arbitrary", "arbitrary"),
        name="fox_attn",
    )(qvt, k, ca, qvt)


def _ffn_body(mixed, wp_ref, x_ref, g_ref, wi_ref, wo_ref, fg_ref, o_ref, final_norm,
              chunks=FFN_CHUNKS, between=()):
    between = list(between)
    gaps = [2 + 3 * len(chunks)]

    def fill_gap():
        for _ in range(-(-len(between) // gaps[0])):
            between.pop(0)()
        gaps[0] -= 1

    tm = x_ref.shape[0]
    halves = (slice(0, tm // 2), slice(tm // 2, tm))
    xs = [x_ref[rows, :] + _dot(mixed[rows], wp_ref[...]) for rows in halves]
    for _ in range(min(LEAD_STAGES, len(between))):
        between.pop(0)()
    gaps[0] -= 1
    hs = [_rmsnorm(x, g_ref[...]).astype(BF16) for x in xs]
    outs = xs
    fill_gap()
    for c0, width in chunks:
        gate = [_dot(h, wi_ref[:, c0:c0 + width]) for h in hs]
        fill_gap()
        up = [_dot(h, wi_ref[:, D_FF + c0:D_FF + c0 + width]) for h in hs]
        fill_gap()
        act = [(g * jax.nn.sigmoid(g) * u).astype(BF16) for g, u in zip(gate, up)]
        outs = [o + _dot(a, wo_ref[c0:c0 + width, :]) for o, a in zip(outs, act)]
        fill_gap()
    assert not between
    for rows, out in zip(halves, outs):
        if final_norm:
            out = _rmsnorm(out, fg_ref[...])
        o_ref[rows, :] = out


def _ffn_kernel(a_ref, wp_ref, x_ref, g_ref, wi_ref, wo_ref, fg_ref, o_ref, *, final_norm):
    mixed = jnp.concatenate([a_ref[hd] for hd in range(a_ref.shape[0])], axis=1)
    _ffn_body(mixed, wp_ref, x_ref, g_ref, wi_ref, wo_ref, fg_ref, o_ref, final_norm)


def _ret_ffn_kernel(q_ref, kt_ref, v_ref, sg_ref, w_ref, qd_ref, kd_ref,
                    wp_ref, x_ref, g_ref, wi_ref, wo_ref, fg_ref, o_ref, gated_ref, state_ref,
                    *, tiles_per_seq, block_decay):
    i = pl.program_id(0)

    @pl.when(i == 0)
    def _():
        gated_ref[...] = jnp.zeros_like(gated_ref)

    @pl.when(i % tiles_per_seq == 0)
    def _():
        state_ref[...] = jnp.zeros_like(state_ref)

    stages = _retention_stages(q_ref, kt_ref, v_ref, sg_ref, w_ref, qd_ref, kd_ref, gated_ref, state_ref,
                               block_decay)
    _ffn_body(gated_ref[...], wp_ref, x_ref, g_ref, wi_ref, wo_ref, fg_ref, o_ref, True,
              chunks=RET_FFN_CHUNKS, between=stages)


def _ffn_specs(w_proj, layer, rows=lambda i: (i, 0)):
    tm = ROW_TILE
    return [
        pl.BlockSpec(w_proj.shape, lambda i: (0, 0), pipeline_mode=pl.Buffered(1)),
        pl.BlockSpec((tm, D_MODEL), rows),
        pl.BlockSpec((1, D_MODEL), lambda i: (0, 0)),
        pl.BlockSpec((None, D_MODEL, 2 * D_FF), lambda i: (layer, 0, 0), pipeline_mode=pl.Buffered(1)),
        pl.BlockSpec((None, D_FF, D_MODEL), lambda i: (layer, 0, 0), pipeline_mode=pl.Buffered(1)),
        pl.BlockSpec((1, D_MODEL), lambda i: (0, 0)),
    ]


def _ffn(mixed, w_proj, x2d, g, w_in_all, w_out_all, layer, final_g, final_norm, name):
    t = x2d.shape[0]
    tm = ROW_TILE
    return pl.pallas_call(
        functools.partial(_ffn_kernel, final_norm=final_norm),
        grid=(t // tm,),
        in_specs=[pl.BlockSpec((mixed.shape[0], tm, mixed.shape[2]), lambda i: (0, i, 0))]
        + _ffn_specs(w_proj, layer),
        out_specs=pl.BlockSpec((tm, D_MODEL), lambda i: (i, 0)),
        out_shape=jax.ShapeDtypeStruct((t, D_MODEL), F32),
        compiler_params=_params("arbitrary"),
        name=name,
    )(mixed, w_proj, x2d, g, w_in_all, w_out_all, final_g)


def _ret_ffn(q, kt, v, sg, w_proj, x2d, g, w_in_all, w_out_all, layer, final_g, seq):
    t = x2d.shape[0]
    tm = ROW_TILE
    r = RET_BLOCK
    n = t // tm
    mask, qd, kd, block_decay = _retention_tables()
    retained = lambda i: (jnp.minimum(i, n - 1), 0)
    finished = lambda i: (jnp.maximum(i - 1, 0), 0)
    const3 = lambda i: (0, 0, 0)
    once = dict(pipeline_mode=pl.Buffered(1))
    return pl.pallas_call(
        functools.partial(_ret_ffn_kernel, tiles_per_seq=seq // tm, block_decay=block_decay),
        grid=(n + 1,),
        in_specs=[
            pl.BlockSpec((tm, D_MODEL), retained),
            pl.BlockSpec((None, D_MODEL, tm), lambda i: (jnp.minimum(i, n - 1), 0, 0)),
            pl.BlockSpec((tm, 2 * D_MODEL), retained),
            pl.BlockSpec((tm, 2 * D_MODEL), retained),
            pl.BlockSpec((RET_HEADS, r, r), const3, **once),
            pl.BlockSpec((RET_HEADS, r, RET_QK_DIM), const3, **once),
            pl.BlockSpec((RET_HEADS, 1, r), const3, **once),
        ]
        + _ffn_specs(w_proj, layer, finished),
        out_specs=pl.BlockSpec((tm, D_MODEL), finished),
        out_shape=jax.ShapeDtypeStruct((t, D_MODEL), F32),
        scratch_shapes=[
            pltpu.VMEM((tm, 2 * D_MODEL), BF16),
            pltpu.VMEM((RET_HEADS, RET_QK_DIM, RET_V_DIM), F32),
        ],
        compiler_params=_params("arbitrary"),
        name="ret_ffn",
    )(q, kt, v, sg, mask, qd, kd, w_proj, x2d, g, w_in_all, w_out_all, final_g)


def _ret_project_head(hd, h, wq_ref, wk_ref, wv_ref, wg_ref, cos_ref, sin_ref, cost_ref, sint_ref,
                      q_ref, kt_ref, v_ref, sg_ref):
    half = RET_QK_DIM // 2
    c0 = hd * RET_QK_DIM
    cos, sin = cos_ref[...], sin_ref[...]
    acc = _dot(h, wq_ref[:, c0:c0 + RET_QK_DIM])
    t1, t2 = acc[:, :half], acc[:, half:]
    q_ref[:, c0:c0 + half] = (t1 * cos - t2 * sin).astype(BF16)
    q_ref[:, c0 + half:c0 + RET_QK_DIM] = (t1 * sin + t2 * cos).astype(BF16)
    cos_t, sin_t = cost_ref[...], sint_ref[...]
    k_scale = RET_QK_DIM ** -0.5
    acc = jnp.transpose(_dot(h, wk_ref[:, c0:c0 + RET_QK_DIM]))
    t1, t2 = acc[:half], acc[half:]
    kt_ref[c0:c0 + half, :] = ((t1 * cos_t - t2 * sin_t) * k_scale).astype(BF16)
    kt_ref[c0 + half:c0 + RET_QK_DIM, :] = ((t1 * sin_t + t2 * cos_t) * k_scale).astype(BF16)
    v0 = hd * RET_V_DIM
    gate = _dot(h, wg_ref[:, v0:v0 + RET_V_DIM])
    sg_ref[:, v0:v0 + RET_V_DIM] = (gate * jax.nn.sigmoid(gate)).astype(BF16)
    v_ref[:, v0:v0 + RET_V_DIM] = _dot(h, wv_ref[:, v0:v0 + RET_V_DIM]).astype(BF16)


def _retention_stages(q_ref, kt_ref, v_ref, sg_ref, w_ref, qd_ref, kd_ref, o_ref, state_ref, block_decay):
    r = RET_BLOCK
    stages = []
    for hd in range(RET_HEADS):
        qk_cols = slice(hd * RET_QK_DIM, (hd + 1) * RET_QK_DIM)
        v_cols = slice(hd * RET_V_DIM, (hd + 1) * RET_V_DIM)
        for blk in range(q_ref.shape[0] // r):
            rows = slice(blk * r, (blk + 1) * r)
            held = {}

            def score(hd=hd, rows=rows, qk_cols=qk_cols, held=held):
                q = q_ref[rows, qk_cols]
                kt = kt_ref[qk_cols, rows]
                held["p"] = (_dot(q, kt) * w_ref[hd]).astype(BF16)
                held["q_dec"] = (q.astype(F32) * qd_ref[hd]).astype(BF16)
                held["kt_dec"] = (kt.astype(F32) * kd_ref[hd]).astype(BF16)

            def mix(hd=hd, rows=rows, v_cols=v_cols, held=held):
                v = v_ref[rows, v_cols]
                state = state_ref[hd]
                held["y"] = _dot(jnp.concatenate([held["p"], held["q_dec"]], axis=1),
                                 jnp.concatenate([v, state.astype(BF16)], axis=0))
                state_ref[hd] = state * block_decay[hd] + _dot(held["kt_dec"], v)

            def norm(rows=rows, v_cols=v_cols, held=held):
                y = held["y"]
                mu = jnp.mean(y, axis=-1, keepdims=True)
                yc = y - mu
                var = jnp.mean(yc * yc, axis=-1, keepdims=True)
                normed = (yc * lax.rsqrt(var + EPS)).astype(BF16)
                o_ref[rows, v_cols] = sg_ref[rows, v_cols] * normed

            stages += [score, mix, norm]
    return stages


def _retention_tables():
    r = RET_BLOCK
    gamma = 1.0 - 2.0 ** (-5.0 - np.arange(RET_HEADS, dtype=np.float64))
    pos = np.arange(r)
    diff = (pos[:, None] - pos[None, :]).astype(np.float64)
    same = (pos[:, None] // CHUNK) == (pos[None, :] // CHUNK)
    earlier = (pos[None, :] // CHUNK) < (pos[:, None] // CHUNK)
    expo = np.where(same, np.abs(diff), diff)
    w = np.where((same | earlier)[None], gamma[:, None, None] ** expo[None], 0.0)
    idx = np.arange(r, dtype=np.float64)
    qd = gamma[:, None] ** (idx[None, :] + 1.0)
    qd = np.broadcast_to(qd[..., None], (RET_HEADS, r, RET_QK_DIM))
    kd = (gamma[:, None] ** (r - 1.0 - idx[None, :]))[:, None, :]
    block_decay = tuple(float(g) ** r for g in gamma)
    f32 = lambda a: np.ascontiguousarray(a, dtype=np.float32)
    return f32(w), f32(qd), f32(kd), block_decay


def _ret_proj_kernel(x_ref, g_ref, wq_ref, wk_ref, wv_ref, wg_ref, cos_ref, sin_ref, cost_ref, sint_ref,
                     q_ref, kt_ref, v_ref, sg_ref):
    h = _rmsnorm(x_ref[...], g_ref[...]).astype(BF16)
    for hd in range(RET_HEADS):
        _ret_project_head(hd, h, wq_ref, wk_ref, wv_ref, wg_ref, cos_ref, sin_ref, cost_ref, sint_ref,
                          q_ref, kt_ref, v_ref, sg_ref)


def _ret_proj(x2d, g, w, cos, sin, seq):
    t = x2d.shape[0]
    tm = ROW_TILE
    tiles_per_seq = seq // tm
    half = RET_QK_DIM // 2
    row = lambda i: (i, 0)
    const = lambda i: (0, 0)
    return pl.pallas_call(
        _ret_proj_kernel,
        grid=(t // tm,),
        in_specs=[
            pl.BlockSpec((tm, D_MODEL), row),
            pl.BlockSpec((1, D_MODEL), const),
            pl.BlockSpec((D_MODEL, D_MODEL), const, pipeline_mode=pl.Buffered(1)),
            pl.BlockSpec((D_MODEL, D_MODEL), lambda i: (0, 1), pipeline_mode=pl.Buffered(1)),
            pl.BlockSpec((D_MODEL, 2 * D_MODEL), lambda i: (0, 1), pipeline_mode=pl.Buffered(1)),
            pl.BlockSpec((D_MODEL, 2 * D_MODEL), lambda i: (0, 2), pipeline_mode=pl.Buffered(1)),
            pl.BlockSpec((tm, half), lambda i: (i % tiles_per_seq, 0)),
            pl.BlockSpec((tm, half), lambda i: (i % tiles_per_seq, 0)),
            pl.BlockSpec((half, tm), lambda i: (0, i % tiles_per_seq)),
            pl.BlockSpec((half, tm), lambda i: (0, i % tiles_per_seq)),
        ],
        out_specs=[
            pl.BlockSpec((tm, D_MODEL), row),
            pl.BlockSpec((None, D_MODEL, tm), lambda i: (i, 0, 0)),
            pl.BlockSpec((tm, 2 * D_MODEL), row),
            pl.BlockSpec((tm, 2 * D_MODEL), row),
        ],
        out_shape=[
            jax.ShapeDtypeStruct((t, D_MODEL), BF16),
            jax.ShapeDtypeStruct((t // tm, D_MODEL, tm), BF16),
            jax.ShapeDtypeStruct((t, 2 * D_MODEL), BF16),
            jax.ShapeDtypeStruct((t, 2 * D_MODEL), BF16),
        ],
        compiler_params=_params("arbitrary"),
        name="ret_proj",
    )(x2d, g, w, w, w, w, cos, sin, np.ascontiguousarray(cos.T), np.ascontiguousarray(sin.T))


def kernel(x, norm_mix, norm_ffn, fox_w_in, fox_b_f, fox_w_out, ret_w_in, ret_w_out,
           ffn_w_in, ffn_w_out, final_norm):
    batch, seq, d = x.shape
    t = batch * seq
    x2d = x.reshape(t, d)
    row = lambda a: a.reshape(1, -1)

    w_in = fox_w_in[0]
    w_qkv = w_in.astype(BF16)
    pad = LANES - C_PARTS * FOX_HEADS
    wf = jnp.pad(jnp.tile(w_in[:, 3 * D_MODEL:], (1, C_PARTS)), ((0, 0), (0, pad))).astype(BF16)
    bf = jnp.pad(jnp.tile(fox_b_f[0], C_PARTS), (0, pad)).reshape(1, LANES)
    k, qvt, ca = _fox_proj(x2d, row(norm_mix[0]), w_qkv, wf, bf, seq)
    attn = _fox_attn(k, ca, qvt, batch, seq)
    ffn_in = ffn_w_in.astype(BF16)
    ffn_out = ffn_w_out.astype(BF16)
    x2d = _ffn(attn, fox_w_out[0].astype(BF16), x2d, row(norm_ffn[0]), ffn_in, ffn_out, 0,
               row(final_norm), False, "ffn0")

    half = RET_QK_DIM // 2
    inv = ROPE_BASE ** (-np.arange(half, dtype=np.float64) / half)
    ang = np.arange(seq, dtype=np.float64)[:, None] * inv[None, :]
    cos, sin = np.cos(ang).astype(np.float32), np.sin(ang).astype(np.float32)
    w_in = ret_w_in[0]
    q, kt, v, sg = _ret_proj(x2d, row(norm_mix[1]), w_in.astype(BF16), cos, sin, seq)
    x2d = _ret_ffn(q, kt, v, sg, ret_w_out[0].astype(BF16), x2d, row(norm_ffn[1]), ffn_in, ffn_out, 1,
                   row(final_norm), seq)
    return x2d.reshape(batch, seq, d)
```

```python
import functools
import math

import jax
import jax.numpy as jnp
import numpy as np
from jax import lax
from jax.experimental import pallas as pl
from jax.experimental.pallas import tpu as pltpu

D_MODEL = 1024
CHUNK = 64
FOX_HEADS = 8
FOX_HEAD_DIM = D_MODEL // FOX_HEADS
RET_HEADS = 4
RET_QK_DIM = D_MODEL // RET_HEADS
RET_V_DIM = 2 * D_MODEL // RET_HEADS
D_FF = -(-8 * D_MODEL // (3 * 256)) * 256
ROPE_BASE = 10000.0
EPS = 1e-6
LOG2E = math.log2(math.e)

F32 = jnp.float32
BF16 = jnp.bfloat16

VMEM_LIMIT_BYTES = 56 * 1024 * 1024
LANES = 128

ROW_TILE = 512
ATTN_Q_TILE = 2 * ROW_TILE
ATTN_K_TILE = ATTN_Q_TILE // 2
RET_BLOCK = 256
FFN_CHUNKS = ((0, 1024), (1024, 1024), (2048, 768))
RET_FFN_CHUNKS = tuple((c0, min(512, D_FF - c0)) for c0 in range(0, D_FF, 512))
LEAD_STAGES = 2
C_PARTS = 3
ONES_ROWS = 16


def _params(*semantics):
    return pltpu.CompilerParams(dimension_semantics=semantics, vmem_limit_bytes=VMEM_LIMIT_BYTES)


def _rmsnorm(x, g):
    return x * lax.rsqrt(jnp.mean(x * x, axis=-1, keepdims=True) + EPS) * g


def _dot(a, b):
    return jnp.dot(a, b, preferred_element_type=F32)


def _cumsum_rows(v):
    n = v.shape[0]
    row = lax.broadcasted_iota(jnp.int32, v.shape, 0)
    shift = 1
    while shift < n:
        rolled = pltpu.roll(v, shift, axis=0)
        v = v + jnp.where(row >= shift, rolled, 0.0)
        shift *= 2
    return v


def _fox_proj_kernel(x_ref, g_ref, wq_ref, wk_ref, wv_ref, wf_ref, bf_ref, k_ref, qvt_ref, ca_ref, carry_ref,
                     *, tiles_per_seq):
    i = pl.program_id(0)
    tm = x_ref.shape[0]
    h = _rmsnorm(x_ref[...], g_ref[...]).astype(BF16)
    logits = _dot(h, wf_ref[...]) + bf_ref[...]
    log_f = jnp.minimum(logits, 0.0) - jnp.log1p(jnp.exp(-jnp.abs(logits)))
    carry = jnp.where(i % tiles_per_seq == 0, 0.0, carry_ref[...])
    c = _cumsum_rows(log_f) + carry
    carry_ref[...] = c[tm - 1:tm, :]
    neg_c = c * (-LOG2E)
    hi = neg_c.astype(BF16)
    rest = neg_c - hi.astype(F32)
    mid = rest.astype(BF16)
    lo = (rest - mid.astype(F32)).astype(BF16)
    lane = lax.broadcasted_iota(jnp.int32, neg_c.shape, 1)
    ca_ref[...] = jnp.where(lane < FOX_HEADS, hi, jnp.where(lane < 2 * FOX_HEADS, mid, lo))

    q_scale = FOX_HEAD_DIM ** -0.5 * LOG2E
    tn = 512
    for n0 in range(0, D_MODEL, tn):
        acc = _dot(h, wk_ref[:, n0:n0 + tn]).astype(BF16)
        for d0 in range(0, tn, FOX_HEAD_DIM):
            k_ref[(n0 + d0) // FOX_HEAD_DIM] = acc[:, d0:d0 + FOX_HEAD_DIM]
    for n0 in range(0, D_MODEL, tn):
        acc = jnp.transpose(_dot(h, wq_ref[:, n0:n0 + tn])) * q_scale
        qvt_ref[n0:n0 + tn, :] = acc.astype(BF16)
    for n0 in range(0, D_MODEL, tn):
        acc = jnp.transpose(_dot(h, wv_ref[:, n0:n0 + tn]))
        qvt_ref[D_MODEL + n0:D_MODEL + n0 + tn, :] = acc.astype(BF16)


def _fox_proj(x2d, g, w_qkv, wf, bf, seq):
    t = x2d.shape[0]
    tm = ROW_TILE
    window = lambda col: pl.BlockSpec((D_MODEL, D_MODEL), lambda i: (0, col), pipeline_mode=pl.Buffered(1))
    return pl.pallas_call(
        functools.partial(_fox_proj_kernel, tiles_per_seq=seq // tm),
        grid=(t // tm,),
        in_specs=[
            pl.BlockSpec((tm, D_MODEL), lambda i: (i, 0)),
            pl.BlockSpec((1, D_MODEL), lambda i: (0, 0)),
            window(0),
            window(1),
            window(2),
            pl.BlockSpec((D_MODEL, LANES), lambda i: (0, 0)),
            pl.BlockSpec((1, LANES), lambda i: (0, 0)),
        ],
        out_specs=[
            pl.BlockSpec((FOX_HEADS, tm, FOX_HEAD_DIM), lambda i: (0, i, 0)),
            pl.BlockSpec((None, 2 * D_MODEL, tm), lambda i: (i, 0, 0)),
            pl.BlockSpec((tm, LANES), lambda i: (i, 0)),
        ],
        out_shape=[
            jax.ShapeDtypeStruct((FOX_HEADS, t, FOX_HEAD_DIM), BF16),
            jax.ShapeDtypeStruct((t // tm, 2 * D_MODEL, tm), BF16),
            jax.ShapeDtypeStruct((t, LANES), BF16),
        ],
        scratch_shapes=[pltpu.VMEM((1, LANES), F32)],
        compiler_params=_params("arbitrary"),
        name="fox_proj",
    )(x2d, g, w_qkv, w_qkv, w_qkv, wf, bf)


def _fox_attn_kernel(qt_ref, k_ref, ca_ref, vt_ref, o_ref, s_ref, smax_ref, m_ref, acc_ref, done_ref):
    head = pl.program_id(1)
    tq = ATTN_Q_TILE
    tk = ATTN_K_TILE
    tv = vt_ref.shape[2]
    nq = (qt_ref.shape[0] * tv) // tq
    assert tq == 2 * tk and tk == tv

    piece = lax.broadcasted_iota(jnp.int32, (LANES, tq), 0)
    pick = jnp.logical_and(piece % FOX_HEADS == head, piece < C_PARTS * FOX_HEADS)
    ones_rows = jnp.where(pick, 1.0, 0.0).astype(BF16)

    def queries(i):
        return jnp.concatenate([jnp.concatenate([qt_ref[2 * i], qt_ref[2 * i + 1]], axis=1), ones_rows], axis=0)

    def scores(qt_ext, row0, c0=0):
        k_ext = jnp.concatenate([k_ref[pl.ds(row0, tk), :], ca_ref[pl.ds(row0, tk), :]], axis=1)
        return _dot(k_ext, qt_ext[:, c0:])

    def update(s, vt, c0=0, diagonal=False, tile_max=None):
        if diagonal:
            key = lax.broadcasted_iota(jnp.int32, s.shape, 0)
            qry = lax.broadcasted_iota(jnp.int32, s.shape, 1)
            s = jnp.where(qry >= key, s, -jnp.inf)
        if tile_max is None:
            tile_max = jnp.max(s, axis=0, keepdims=True)
        m_old = m_ref[:, c0:]
        m_new = jnp.maximum(m_old, tile_max)
        alpha = jnp.exp2(m_old - m_new)
        p = jnp.exp2(s - m_new).astype(BF16)
        vt_ext = jnp.concatenate([vt, jnp.ones((ONES_ROWS, vt.shape[1]), BF16)], axis=0)
        acc_ref[:, c0:] = alpha * acc_ref[:, c0:] + _dot(vt_ext, p)
        m_ref[:, c0:] = m_new

    def stash(slot, qt_ext, row0):
        s = scores(qt_ext, row0)
        s_ref[slot] = s
        smax_ref[slot] = jnp.max(s, axis=0, keepdims=True)

    def query_tile(i, head_slot, next_slot, odd):
        qt_ext = queries(i)
        m_ref[...] = jnp.full_like(m_ref, -jnp.inf)
        acc_ref[...] = jnp.zeros_like(acc_ref)

        def body(j, carry):
            base = pl.multiple_of(j * tq, tq)
            stash(1, qt_ext, base + tk)
            update(s_ref[head_slot], vt_ref[2 * j], tile_max=smax_ref[head_slot])
            stash(head_slot, qt_ext, base + tq)
            update(s_ref[1], vt_ref[2 * j + 1], tile_max=smax_ref[1])
            return carry

        def two_steps(jj, carry):
            body(2 * jj, carry)
            return body(2 * jj + 1, carry)

        lax.fori_loop(0, i // 2, two_steps, 0)
        if odd:
            body(i - 1, 0)

        base = pl.multiple_of(i * tq, tq)
        s_last = scores(qt_ext, base + tk, tk)
        qt_next = queries(jnp.minimum(i + 1, nq - 1))
        stash(next_slot, qt_next, 0)
        finish(pl.multiple_of(jnp.maximum(i - 1, 0) * tq, tq))
        update(s_ref[head_slot], vt_ref[2 * i], 0, diagonal=True)
        update(s_last, vt_ref[2 * i + 1], tk, diagonal=True)
        done_ref[...] = acc_ref[...]

    def two_query_tiles(pair, carry):
        query_tile(2 * pair, 0, 2, odd=False)
        query_tile(2 * pair + 1, 2, 0, odd=True)
        return carry

    def finish(row0):
        done = done_ref[...]
        out = done[:FOX_HEAD_DIM] / done[FOX_HEAD_DIM:FOX_HEAD_DIM + 1]
        o_ref[pl.ds(row0, tq), :] = jnp.transpose(out).astype(o_ref.dtype)

    done_ref[...] = jnp.ones_like(done_ref)
    stash(0, queries(0), 0)
    assert nq % 2 == 0
    lax.fori_loop(0, nq // 2, two_query_tiles, 0)
    finish((nq - 1) * tq)


def _fox_attn(k, ca, qvt, batch, seq):
    t = k.shape[1]
    tq = ATTN_Q_TILE
    tk = ATTN_K_TILE
    tv = qvt.shape[2]
    hb = D_MODEL // FOX_HEAD_DIM
    return pl.pallas_call(
        _fox_attn_kernel,
        grid=(batch, FOX_HEADS),
        in_specs=[
            pl.BlockSpec((seq // tv, FOX_HEAD_DIM, tv), lambda b, h: (b, h, 0)),
            pl.BlockSpec((None, seq, FOX_HEAD_DIM), lambda b, h: (h, b, 0)),
            pl.BlockSpec((seq, LANES), lambda b, h: (b, 0)),
            pl.BlockSpec((seq // tv, FOX_HEAD_DIM, tv), lambda b, h: (b, hb + h, 0)),
        ],
        out_specs=pl.BlockSpec((None, seq, FOX_HEAD_DIM), lambda b, h: (h, b, 0)),
        out_shape=jax.ShapeDtypeStruct((FOX_HEADS, t, FOX_HEAD_DIM), BF16),
        scratch_shapes=[
            pltpu.VMEM((3, tk, tq), F32),
            pltpu.VMEM((3, 1, tq), F32),
            pltpu.VMEM((1, tq), F32),
            pltpu.VMEM((FOX_HEAD_DIM + ONES_ROWS, tq), F32),
            pltpu.VMEM((FOX_HEAD_DIM + ONES_ROWS, tq), F32),
        ],
        compiler_params=_params("arbitrary", "arbitrary"),
        name="fox_attn",
    )(qvt, k, ca, qvt)


def _ffn_body(mixed, wp_ref, x_ref, g_ref, wi_ref, wo_ref, fg_ref, o_ref, final_norm,
              chunks=FFN_CHUNKS, between=()):
    between = list(between)
    gaps = [2 + 3 * len(chunks)]

    def fill_gap():
        for _ in range(-(-len(between) // gaps[0])):
            between.pop(0)()
        gaps[0] -= 1

    tm = x_ref.shape[0]
    halves = (slice(0, tm // 2), slice(tm // 2, tm))
    xs = [x_ref[rows, :] + _dot(mixed[rows], wp_ref[...]) for rows in halves]
    for _ in range(min(LEAD_STAGES, len(between))):
        between.pop(0)()
    gaps[0] -= 1
    hs = [_rmsnorm(x, g_ref[...]).astype(BF16) for x in xs]
    outs = xs
    fill_gap()
    for c0, width in chunks:
        gate = [_dot(h, wi_ref[:, c0:c0 + width]) for h in hs]
        fill_gap()
        up = [_dot(h, wi_ref[:, D_FF + c0:D_FF + c0 + width]) for h in hs]
        fill_gap()
        act = [(g * jax.nn.sigmoid(g) * u).astype(BF16) for g, u in zip(gate, up)]
        outs = [o + _dot(a, wo_ref[c0:c0 + width, :]) for o, a in zip(outs, act)]
        fill_gap()
    assert not between
    for rows, out in zip(halves, outs):
        if final_norm:
            out = _rmsnorm(out, fg_ref[...])
        o_ref[rows, :] = out


def _ffn_kernel(a_ref, wp_ref, x_ref, g_ref, wi_ref, wo_ref, fg_ref, o_ref, *, final_norm):
    mixed = jnp.concatenate([a_ref[hd] for hd in range(a_ref.shape[0])], axis=1)
    _ffn_body(mixed, wp_ref, x_ref, g_ref, wi_ref, wo_ref, fg_ref, o_ref, final_norm)


def _ret_ffn_kernel(q_ref, kt_ref, v_ref, sg_ref, w_ref, qd_ref, kd_ref,
                    wp_ref, x_ref, g_ref, wi_ref, wo_ref, fg_ref, o_ref, gated_ref, state_ref,
                    *, tiles_per_seq, block_decay):
    i = pl.program_id(0)

    @pl.when(i == 0)
    def _():
        gated_ref[...] = jnp.zeros_like(gated_ref)

    @pl.when(i % tiles_per_seq == 0)
    def _():
        state_ref[...] = jnp.zeros_like(state_ref)

    stages = _retention_stages(q_ref, kt_ref, v_ref, sg_ref, w_ref, qd_ref, kd_ref, gated_ref, state_ref,
                               block_decay)
    _ffn_body(gated_ref[...], wp_ref, x_ref, g_ref, wi_ref, wo_ref, fg_ref, o_ref, True,
              chunks=RET_FFN_CHUNKS, between=stages)


def _ffn_specs(w_proj, layer, rows=lambda i: (i, 0)):
    tm = ROW_TILE
    return [
        pl.BlockSpec(w_proj.shape, lambda i: (0, 0), pipeline_mode=pl.Buffered(1)),
        pl.BlockSpec((tm, D_MODEL), rows),
        pl.BlockSpec((1, D_MODEL), lambda i: (0, 0)),
        pl.BlockSpec((None, D_MODEL, 2 * D_FF), lambda i: (layer, 0, 0), pipeline_mode=pl.Buffered(1)),
        pl.BlockSpec((None, D_FF, D_MODEL), lambda i: (layer, 0, 0), pipeline_mode=pl.Buffered(1)),
        pl.BlockSpec((1, D_MODEL), lambda i: (0, 0)),
    ]


def _ffn(mixed, w_proj, x2d, g, w_in_all, w_out_all, layer, final_g, final_norm, name):
    t = x2d.shape[0]
    tm = ROW_TILE
    return pl.pallas_call(
        functools.partial(_ffn_kernel, final_norm=final_norm),
        grid=(t // tm,),
        in_specs=[pl.BlockSpec((mixed.shape[0], tm, mixed.shape[2]), lambda i: (0, i, 0))]
        + _ffn_specs(w_proj, layer),
        out_specs=pl.BlockSpec((tm, D_MODEL), lambda i: (i, 0)),
        out_shape=jax.ShapeDtypeStruct((t, D_MODEL), F32),
        compiler_params=_params("arbitrary"),
        name=name,
    )(mixed, w_proj, x2d, g, w_in_all, w_out_all, final_g)


def _ret_ffn(q, kt, v, sg, w_proj, x2d, g, w_in_all, w_out_all, layer, final_g, seq):
    t = x2d.shape[0]
    tm = ROW_TILE
    r = RET_BLOCK
    n = t // tm
    mask, qd, kd, block_decay = _retention_tables()
    retained = lambda i: (jnp.minimum(i, n - 1), 0)
    finished = lambda i: (jnp.maximum(i - 1, 0), 0)
    const3 = lambda i: (0, 0, 0)
    once = dict(pipeline_mode=pl.Buffered(1))
    return pl.pallas_call(
        functools.partial(_ret_ffn_kernel, tiles_per_seq=seq // tm, block_decay=block_decay),
        grid=(n + 1,),
        in_specs=[
            pl.BlockSpec((tm, D_MODEL), retained),
            pl.BlockSpec((None, D_MODEL, tm), lambda i: (jnp.minimum(i, n - 1), 0, 0)),
            pl.BlockSpec((tm, 2 * D_MODEL), retained),
            pl.BlockSpec((tm, 2 * D_MODEL), retained),
            pl.BlockSpec((RET_HEADS, r, r), const3, **once),
            pl.BlockSpec((RET_HEADS, r, RET_QK_DIM), const3, **once),
            pl.BlockSpec((RET_HEADS, 1, r), const3, **once),
        ]
        + _ffn_specs(w_proj, layer, finished),
        out_specs=pl.BlockSpec((tm, D_MODEL), finished),
        out_shape=jax.ShapeDtypeStruct((t, D_MODEL), F32),
        scratch_shapes=[
            pltpu.VMEM((tm, 2 * D_MODEL), BF16),
            pltpu.VMEM((RET_HEADS, RET_QK_DIM, RET_V_DIM), F32),
        ],
        compiler_params=_params("arbitrary"),
        name="ret_ffn",
    )(q, kt, v, sg, mask, qd, kd, w_proj, x2d, g, w_in_all, w_out_all, final_g)


def _ret_project_head(hd, rows, h, wq_ref, wk_ref, wv_ref, wg_ref, cos_ref, sin_ref, cost_ref, sint_ref,
                      q_ref, kt_ref, v_ref, sg_ref):
    half = RET_QK_DIM // 2
    c0 = hd * RET_QK_DIM
    cos, sin = cos_ref[rows, :], sin_ref[rows, :]
    acc = _dot(h, wq_ref[:, c0:c0 + RET_QK_DIM])
    t1, t2 = acc[:, :half], acc[:, half:]
    q_ref[rows, c0:c0 + half] = (t1 * cos - t2 * sin).astype(BF16)
    q_ref[rows, c0 + half:c0 + RET_QK_DIM] = (t1 * sin + t2 * cos).astype(BF16)
    cos_t, sin_t = cost_ref[:, rows], sint_ref[:, rows]
    k_scale = RET_QK_DIM ** -0.5
    acc = jnp.transpose(_dot(h, wk_ref[:, c0:c0 + RET_QK_DIM]))
    t1, t2 = acc[:half], acc[half:]
    kt_ref[c0:c0 + half, rows] = ((t1 * cos_t - t2 * sin_t) * k_scale).astype(BF16)
    kt_ref[c0 + half:c0 + RET_QK_DIM, rows] = ((t1 * sin_t + t2 * cos_t) * k_scale).astype(BF16)
    v0 = hd * RET_V_DIM
    gate = _dot(h, wg_ref[:, v0:v0 + RET_V_DIM])
    sg_ref[rows, v0:v0 + RET_V_DIM] = (gate * jax.nn.sigmoid(gate)).astype(BF16)
    v_ref[rows, v0:v0 + RET_V_DIM] = _dot(h, wv_ref[:, v0:v0 + RET_V_DIM]).astype(BF16)


def _retention_stages(q_ref, kt_ref, v_ref, sg_ref, w_ref, qd_ref, kd_ref, o_ref, state_ref, block_decay):
    r = RET_BLOCK
    stages = []
    for hd in range(RET_HEADS):
        qk_cols = slice(hd * RET_QK_DIM, (hd + 1) * RET_QK_DIM)
        v_cols = slice(hd * RET_V_DIM, (hd + 1) * RET_V_DIM)
        for blk in range(q_ref.shape[0] // r):
            rows = slice(blk * r, (blk + 1) * r)
            held = {}

            def score(hd=hd, rows=rows, qk_cols=qk_cols, held=held):
                q = q_ref[rows, qk_cols]
                kt = kt_ref[qk_cols, rows]
                held["p"] = (_dot(q, kt) * w_ref[hd]).astype(BF16)
                held["q_dec"] = (q.astype(F32) * qd_ref[hd]).astype(BF16)
                held["kt_dec"] = (kt.astype(F32) * kd_ref[hd]).astype(BF16)

            def mix(hd=hd, rows=rows, v_cols=v_cols, held=held):
                v = v_ref[rows, v_cols]
                state = state_ref[hd]
                held["y"] = _dot(jnp.concatenate([held["p"], held["q_dec"]], axis=1),
                                 jnp.concatenate([v, state.astype(BF16)], axis=0))
                state_ref[hd] = state * block_decay[hd] + _dot(held["kt_dec"], v)

            def norm(rows=rows, v_cols=v_cols, held=held):
                y = held["y"]
                mu = jnp.mean(y, axis=-1, keepdims=True)
                yc = y - mu
                var = jnp.mean(yc * yc, axis=-1, keepdims=True)
                normed = (yc * lax.rsqrt(var + EPS)).astype(BF16)
                o_ref[rows, v_cols] = sg_ref[rows, v_cols] * normed

            stages += [score, mix, norm]
    return stages


def _retention_tables():
    r = RET_BLOCK
    gamma = 1.0 - 2.0 ** (-5.0 - np.arange(RET_HEADS, dtype=np.float64))
    pos = np.arange(r)
    diff = (pos[:, None] - pos[None, :]).astype(np.float64)
    same = (pos[:, None] // CHUNK) == (pos[None, :] // CHUNK)
    earlier = (pos[None, :] // CHUNK) < (pos[:, None] // CHUNK)
    expo = np.where(same, np.abs(diff), diff)
    w = np.where((same | earlier)[None], gamma[:, None, None] ** expo[None], 0.0)
    idx = np.arange(r, dtype=np.float64)
    qd = gamma[:, None] ** (idx[None, :] + 1.0)
    qd = np.broadcast_to(qd[..., None], (RET_HEADS, r, RET_QK_DIM))
    kd = (gamma[:, None] ** (r - 1.0 - idx[None, :]))[:, None, :]
    block_decay = tuple(float(g) ** r for g in gamma)
    f32 = lambda a: np.ascontiguousarray(a, dtype=np.float32)
    return f32(w), f32(qd), f32(kd), block_decay


def _ret_proj_kernel(x_ref, g_ref, wq_ref, wk_ref, wv_ref, wg_ref, cos_ref, sin_ref, cost_ref, sint_ref,
                     q_ref, kt_ref, v_ref, sg_ref):
    tm = x_ref.shape[0]
    halves = (slice(0, tm // 2), slice(tm // 2, tm))
    hs = [_rmsnorm(x_ref[rows, :], g_ref[...]).astype(BF16) for rows in halves]
    for hd in range(RET_HEADS):
        for rows, h in zip(halves, hs):
            _ret_project_head(hd, rows, h, wq_ref, wk_ref, wv_ref, wg_ref, cos_ref, sin_ref, cost_ref,
                              sint_ref, q_ref, kt_ref, v_ref, sg_ref)


def _ret_proj(x2d, g, w, cos, sin, seq):
    t = x2d.shape[0]
    tm = ROW_TILE
    tiles_per_seq = seq // tm
    half = RET_QK_DIM // 2
    row = lambda i: (i, 0)
    const = lambda i: (0, 0)
    return pl.pallas_call(
        _ret_proj_kernel,
        grid=(t // tm,),
        in_specs=[
            pl.BlockSpec((tm, D_MODEL), row),
            pl.BlockSpec((1, D_MODEL), const),
            pl.BlockSpec((D_MODEL, D_MODEL), const, pipeline_mode=pl.Buffered(1)),
            pl.BlockSpec((D_MODEL, D_MODEL), lambda i: (0, 1), pipeline_mode=pl.Buffered(1)),
            pl.BlockSpec((D_MODEL, 2 * D_MODEL), lambda i: (0, 1), pipeline_mode=pl.Buffered(1)),
            pl.BlockSpec((D_MODEL, 2 * D_MODEL), lambda i: (0, 2), pipeline_mode=pl.Buffered(1)),
            pl.BlockSpec((tm, half), lambda i: (i % tiles_per_seq, 0)),
            pl.BlockSpec((tm, half), lambda i: (i % tiles_per_seq, 0)),
            pl.BlockSpec((half, tm), lambda i: (0, i % tiles_per_seq)),
            pl.BlockSpec((half, tm), lambda i: (0, i % tiles_per_seq)),
        ],
        out_specs=[
            pl.BlockSpec((tm, D_MODEL), row),
            pl.BlockSpec((None, D_MODEL, tm), lambda i: (i, 0, 0)),
            pl.BlockSpec((tm, 2 * D_MODEL), row),
            pl.BlockSpec((tm, 2 * D_MODEL), row),
        ],
        out_shape=[
            jax.ShapeDtypeStruct((t, D_MODEL), BF16),
            jax.ShapeDtypeStruct((t // tm, D_MODEL, tm), BF16),
            jax.ShapeDtypeStruct((t, 2 * D_MODEL), BF16),
            jax.ShapeDtypeStruct((t, 2 * D_MODEL), BF16),
        ],
        compiler_params=_params("arbitrary"),
        name="ret_proj",
    )(x2d, g, w, w, w, w, cos, sin, np.ascontiguousarray(cos.T), np.ascontiguousarray(sin.T))


def kernel(x, norm_mix, norm_ffn, fox_w_in, fox_b_f, fox_w_out, ret_w_in, ret_w_out,
           ffn_w_in, ffn_w_out, final_norm):
    batch, seq, d = x.shape
    t = batch * seq
    x2d = x.reshape(t, d)
    row = lambda a: a.reshape(1, -1)

    w_in = fox_w_in[0]
    w_qkv = w_in.astype(BF16)
    pad = LANES - C_PARTS * FOX_HEADS
    wf = jnp.pad(jnp.tile(w_in[:, 3 * D_MODEL:], (1, C_PARTS)), ((0, 0), (0, pad))).astype(BF16)
    bf = jnp.pad(jnp.tile(fox_b_f[0], C_PARTS), (0, pad)).reshape(1, LANES)
    k, qvt, ca = _fox_proj(x2d, row(norm_mix[0]), w_qkv, wf, bf, seq)
    attn = _fox_attn(k, ca, qvt, batch, seq)
    ffn_in = ffn_w_in.astype(BF16)
    ffn_out = ffn_w_out.astype(BF16)
    x2d = _ffn(attn, fox_w_out[0].astype(BF16), x2d, row(norm_ffn[0]), ffn_in, ffn_out, 0,
               row(final_norm), False, "ffn0")

    half = RET_QK_DIM // 2
    inv = ROPE_BASE ** (-np.arange(half, dtype=np.float64) / half)
    ang = np.arange(seq, dtype=np.float64)[:, None] * inv[None, :]
    cos, sin = np.cos(ang).astype(np.float32), np.sin(ang).astype(np.float32)
    w_in = ret_w_in[0]
    q, kt, v, sg = _ret_proj(x2d, row(norm_mix[1]), w_in.astype(BF16), cos, sin, seq)
    x2d = _ret_ffn(q, kt, v, sg, ret_w_out[0].astype(BF16), x2d, row(norm_ffn[1]), ffn_in, ffn_out, 1,
                   row(final_norm), seq)
    return x2d.reshape(batch, seq, d)
```

```python
import functools
import math

import jax
import jax.numpy as jnp
import numpy as np
from jax import lax
from jax.experimental import pallas as pl
from jax.experimental.pallas import tpu as pltpu

D_MODEL = 1024
CHUNK = 64
FOX_HEADS = 8
FOX_HEAD_DIM = D_MODEL // FOX_HEADS
RET_HEADS = 4
RET_QK_DIM = D_MODEL // RET_HEADS
RET_V_DIM = 2 * D_MODEL // RET_HEADS
D_FF = -(-8 * D_MODEL // (3 * 256)) * 256
ROPE_BASE = 10000.0
EPS = 1e-6
LOG2E = math.log2(math.e)

F32 = jnp.float32
BF16 = jnp.bfloat16

VMEM_LIMIT_BYTES = 56 * 1024 * 1024
LANES = 128

ROW_TILE = 512
ATTN_Q_TILE = 2 * ROW_TILE
ATTN_K_TILE = ATTN_Q_TILE // 2
RET_BLOCK = 256
FFN_CHUNKS = ((0, 1024), (1024, 1024), (2048, 768))
RET_FFN_CHUNKS = tuple((c0, min(768, D_FF - c0)) for c0 in range(0, D_FF, 768))
LEAD_STAGES = 2
C_PARTS = 3
ONES_ROWS = 16


def _params(*semantics):
    return pltpu.CompilerParams(dimension_semantics=semantics, vmem_limit_bytes=VMEM_LIMIT_BYTES)


def _rmsnorm(x, g):
    return x * lax.rsqrt(jnp.mean(x * x, axis=-1, keepdims=True) + EPS) * g


def _dot(a, b):
    return jnp.dot(a, b, preferred_element_type=F32)


def _cumsum_rows(v):
    n = v.shape[0]
    row = lax.broadcasted_iota(jnp.int32, v.shape, 0)
    shift = 1
    while shift < n:
        rolled = pltpu.roll(v, shift, axis=0)
        v = v + jnp.where(row >= shift, rolled, 0.0)
        shift *= 2
    return v


def _fox_proj_kernel(x_ref, g_ref, wq_ref, wk_ref, wv_ref, wf_ref, bf_ref, k_ref, qvt_ref, ca_ref, carry_ref,
                     *, tiles_per_seq):
    i = pl.program_id(0)
    tm = x_ref.shape[0]
    h = _rmsnorm(x_ref[...], g_ref[...]).astype(BF16)
    logits = _dot(h, wf_ref[...]) + bf_ref[...]
    log_f = jnp.minimum(logits, 0.0) - jnp.log1p(jnp.exp(-jnp.abs(logits)))
    carry = jnp.where(i % tiles_per_seq == 0, 0.0, carry_ref[...])
    c = _cumsum_rows(log_f) + carry
    carry_ref[...] = c[tm - 1:tm, :]
    neg_c = c * (-LOG2E)
    hi = neg_c.astype(BF16)
    rest = neg_c - hi.astype(F32)
    mid = rest.astype(BF16)
    lo = (rest - mid.astype(F32)).astype(BF16)
    lane = lax.broadcasted_iota(jnp.int32, neg_c.shape, 1)
    ca_ref[...] = jnp.where(lane < FOX_HEADS, hi, jnp.where(lane < 2 * FOX_HEADS, mid, lo))

    q_scale = FOX_HEAD_DIM ** -0.5 * LOG2E
    tn = 512
    for n0 in range(0, D_MODEL, tn):
        acc = _dot(h, wk_ref[:, n0:n0 + tn]).astype(BF16)
        for d0 in range(0, tn, FOX_HEAD_DIM):
            k_ref[(n0 + d0) // FOX_HEAD_DIM] = acc[:, d0:d0 + FOX_HEAD_DIM]
    for n0 in range(0, D_MODEL, tn):
        acc = jnp.transpose(_dot(h, wq_ref[:, n0:n0 + tn])) * q_scale
        qvt_ref[n0:n0 + tn, :] = acc.astype(BF16)
    for n0 in range(0, D_MODEL, tn):
        acc = jnp.transpose(_dot(h, wv_ref[:, n0:n0 + tn]))
        qvt_ref[D_MODEL + n0:D_MODEL + n0 + tn, :] = acc.astype(BF16)


def _fox_proj(x2d, g, w_qkv, wf, bf, seq):
    t = x2d.shape[0]
    tm = ROW_TILE
    window = lambda col: pl.BlockSpec((D_MODEL, D_MODEL), lambda i: (0, col), pipeline_mode=pl.Buffered(1))
    return pl.pallas_call(
        functools.partial(_fox_proj_kernel, tiles_per_seq=seq // tm),
        grid=(t // tm,),
        in_specs=[
            pl.BlockSpec((tm, D_MODEL), lambda i: (i, 0)),
            pl.BlockSpec((1, D_MODEL), lambda i: (0, 0)),
            window(0),
            window(1),
            window(2),
            pl.BlockSpec((D_MODEL, LANES), lambda i: (0, 0)),
            pl.BlockSpec((1, LANES), lambda i: (0, 0)),
        ],
        out_specs=[
            pl.BlockSpec((FOX_HEADS, tm, FOX_HEAD_DIM), lambda i: (0, i, 0)),
            pl.BlockSpec((None, 2 * D_MODEL, tm), lambda i: (i, 0, 0)),
            pl.BlockSpec((tm, LANES), lambda i: (i, 0)),
        ],
        out_shape=[
            jax.ShapeDtypeStruct((FOX_HEADS, t, FOX_HEAD_DIM), BF16),
            jax.ShapeDtypeStruct((t // tm, 2 * D_MODEL, tm), BF16),
            jax.ShapeDtypeStruct((t, LANES), BF16),
        ],
        scratch_shapes=[pltpu.VMEM((1, LANES), F32)],
        compiler_params=_params("arbitrary"),
        name="fox_proj",
    )(x2d, g, w_qkv, w_qkv, w_qkv, wf, bf)


def _fox_attn_kernel(qt_ref, k_ref, ca_ref, vt_ref, o_ref, s_ref, smax_ref, m_ref, acc_a_ref, acc_b_ref):
    head = pl.program_id(1)
    tq = ATTN_Q_TILE
    tk = ATTN_K_TILE
    tv = vt_ref.shape[2]
    nq = (qt_ref.shape[0] * tv) // tq
    assert tq == 2 * tk and tk == tv

    piece = lax.broadcasted_iota(jnp.int32, (LANES, tq), 0)
    pick = jnp.logical_and(piece % FOX_HEADS == head, piece < C_PARTS * FOX_HEADS)
    ones_rows = jnp.where(pick, 1.0, 0.0).astype(BF16)

    def queries(i):
        return jnp.concatenate([jnp.concatenate([qt_ref[2 * i], qt_ref[2 * i + 1]], axis=1), ones_rows], axis=0)

    def scores(qt_ext, row0, c0=0):
        k_ext = jnp.concatenate([k_ref[pl.ds(row0, tk), :], ca_ref[pl.ds(row0, tk), :]], axis=1)
        return _dot(k_ext, qt_ext[:, c0:])

    def update(s, vt, c0=0, diagonal=False, tile_max=None):
        if diagonal:
            key = lax.broadcasted_iota(jnp.int32, s.shape, 0)
            qry = lax.broadcasted_iota(jnp.int32, s.shape, 1)
            s = jnp.where(qry >= key, s, -jnp.inf)
        if tile_max is None:
            tile_max = jnp.max(s, axis=0, keepdims=True)
        m_old = m_ref[:, c0:]
        m_new = jnp.maximum(m_old, tile_max)
        alpha = jnp.exp2(m_old - m_new)
        p = jnp.exp2(s - m_new).astype(BF16)
        vt_ext = jnp.concatenate([vt, jnp.ones((ONES_ROWS, vt.shape[1]), BF16)], axis=0)
        acc = accumulator[0]
        acc[:, c0:] = alpha * acc[:, c0:] + _dot(vt_ext, p)
        m_ref[:, c0:] = m_new

    def stash(slot, qt_ext, row0):
        s = scores(qt_ext, row0)
        s_ref[slot] = s
        smax_ref[slot] = jnp.max(s, axis=0, keepdims=True)

    def query_tile(i, head_slot, next_slot, odd):
        qt_ext = queries(i)
        acc, previous = (acc_b_ref, acc_a_ref) if odd else (acc_a_ref, acc_b_ref)
        accumulator[0] = acc
        m_ref[...] = jnp.full_like(m_ref, -jnp.inf)
        acc[...] = jnp.zeros_like(acc)

        def body(j, carry):
            base = pl.multiple_of(j * tq, tq)
            stash(1, qt_ext, base + tk)
            update(s_ref[head_slot], vt_ref[2 * j], tile_max=smax_ref[head_slot])
            stash(head_slot, qt_ext, base + tq)
            update(s_ref[1], vt_ref[2 * j + 1], tile_max=smax_ref[1])
            return carry

        def two_steps(jj, carry):
            body(2 * jj, carry)
            return body(2 * jj + 1, carry)

        lax.fori_loop(0, i // 2, two_steps, 0)
        if odd:
            body(i - 1, 0)

        base = pl.multiple_of(i * tq, tq)
        s_last = scores(qt_ext, base + tk, tk)
        qt_next = queries(jnp.minimum(i + 1, nq - 1))
        stash(next_slot, qt_next, 0)
        finish(previous, pl.multiple_of(jnp.maximum(i - 1, 0) * tq, tq))
        update(s_ref[head_slot], vt_ref[2 * i], 0, diagonal=True)
        update(s_last, vt_ref[2 * i + 1], tk, diagonal=True)

    def two_query_tiles(pair, carry):
        query_tile(2 * pair, 0, 2, odd=False)
        query_tile(2 * pair + 1, 2, 0, odd=True)
        return carry

    def finish(acc, row0):
        done = acc[...]
        out = done[:FOX_HEAD_DIM] / done[FOX_HEAD_DIM:FOX_HEAD_DIM + 1]
        o_ref[pl.ds(row0, tq), :] = jnp.transpose(out).astype(o_ref.dtype)

    accumulator = [acc_a_ref]
    acc_b_ref[...] = jnp.ones_like(acc_b_ref)
    stash(0, queries(0), 0)
    assert nq % 2 == 0
    lax.fori_loop(0, nq // 2, two_query_tiles, 0)
    finish(acc_b_ref, (nq - 1) * tq)


def _fox_attn(k, ca, qvt, batch, seq):
    t = k.shape[1]
    tq = ATTN_Q_TILE
    tk = ATTN_K_TILE
    tv = qvt.shape[2]
    hb = D_MODEL // FOX_HEAD_DIM
    return pl.pallas_call(
        _fox_attn_kernel,
        grid=(batch, FOX_HEADS),
        in_specs=[
            pl.BlockSpec((seq // tv, FOX_HEAD_DIM, tv), lambda b, h: (b, h, 0)),
            pl.BlockSpec((None, seq, FOX_HEAD_DIM), lambda b, h: (h, b, 0)),
            pl.BlockSpec((seq, LANES), lambda b, h: (b, 0)),
            pl.BlockSpec((seq // tv, FOX_HEAD_DIM, tv), lambda b, h: (b, hb + h, 0)),
        ],
        out_specs=pl.BlockSpec((None, seq, FOX_HEAD_DIM), lambda b, h: (h, b, 0)),
        out_shape=jax.ShapeDtypeStruct((FOX_HEADS, t, FOX_HEAD_DIM), BF16),
        scratch_shapes=[
            pltpu.VMEM((3, tk, tq), F32),
            pltpu.VMEM((3, 1, tq), F32),
            pltpu.VMEM((1, tq), F32),
            pltpu.VMEM((FOX_HEAD_DIM + ONES_ROWS, tq), F32),
            pltpu.VMEM((FOX_HEAD_DIM + ONES_ROWS, tq), F32),
        ],
        compiler_params=_params("arbitrary", "arbitrary"),
        name="fox_attn",
    )(qvt, k, ca, qvt)


def _ffn_body(mixed, wp_ref, x_ref, g_ref, wi_ref, wo_ref, fg_ref, o_ref, final_norm,
              chunks=FFN_CHUNKS, between=()):
    between = list(between)
    gaps = [2 + 3 * len(chunks)]

    def fill_gap():
        for _ in range(-(-len(between) // gaps[0])):
            between.pop(0)()
        gaps[0] -= 1

    tm = x_ref.shape[0]
    halves = (slice(0, tm // 2), slice(tm // 2, tm))
    xs = [x_ref[rows, :] + _dot(mixed[rows], wp_ref[...]) for rows in halves]
    for _ in range(min(LEAD_STAGES, len(between))):
        between.pop(0)()
    gaps[0] -= 1
    hs = [_rmsnorm(x, g_ref[...]).astype(BF16) for x in xs]
    outs = xs
    fill_gap()
    for c0, width in chunks:
        gate = [_dot(h, wi_ref[:, c0:c0 + width]) for h in hs]
        fill_gap()
        up = [_dot(h, wi_ref[:, D_FF + c0:D_FF + c0 + width]) for h in hs]
        fill_gap()
        act = [(g * jax.nn.sigmoid(g) * u).astype(BF16) for g, u in zip(gate, up)]
        outs = [o + _dot(a, wo_ref[c0:c0 + width, :]) for o, a in zip(outs, act)]
        fill_gap()
    assert not between
    for rows, out in zip(halves, outs):
        if final_norm:
            out = _rmsnorm(out, fg_ref[...])
        o_ref[rows, :] = out


def _ffn_kernel(a_ref, wp_ref, x_ref, g_ref, wi_ref, wo_ref, fg_ref, o_ref, *, final_norm):
    mixed = jnp.concatenate([a_ref[hd] for hd in range(a_ref.shape[0])], axis=1)
    _ffn_body(mixed, wp_ref, x_ref, g_ref, wi_ref, wo_ref, fg_ref, o_ref, final_norm)


def _ret_ffn_kernel(q_ref, kt_ref, v_ref, sg_ref, w_ref, qd_ref, kd_ref,
                    wp_ref, x_ref, g_ref, wi_ref, wo_ref, fg_ref, o_ref, gated_ref, state_ref,
                    *, tiles_per_seq, block_decay):
    i = pl.program_id(0)

    @pl.when(i == 0)
    def _():
        gated_ref[...] = jnp.zeros_like(gated_ref)

    @pl.when(i % tiles_per_seq == 0)
    def _():
        state_ref[...] = jnp.zeros_like(state_ref)

    stages = _retention_stages(q_ref, kt_ref, v_ref, sg_ref, w_ref, qd_ref, kd_ref, gated_ref, state_ref,
                               block_decay)
    _ffn_body(gated_ref[...], wp_ref, x_ref, g_ref, wi_ref, wo_ref, fg_ref, o_ref, True,
              chunks=RET_FFN_CHUNKS, between=stages)


def _ffn_specs(w_proj, layer, rows=lambda i: (i, 0)):
    tm = ROW_TILE
    return [
        pl.BlockSpec(w_proj.shape, lambda i: (0, 0), pipeline_mode=pl.Buffered(1)),
        pl.BlockSpec((tm, D_MODEL), rows),
        pl.BlockSpec((1, D_MODEL), lambda i: (0, 0)),
        pl.BlockSpec((None, D_MODEL, 2 * D_FF), lambda i: (layer, 0, 0), pipeline_mode=pl.Buffered(1)),
        pl.BlockSpec((None, D_FF, D_MODEL), lambda i: (layer, 0, 0), pipeline_mode=pl.Buffered(1)),
        pl.BlockSpec((1, D_MODEL), lambda i: (0, 0)),
    ]


def _ffn(mixed, w_proj, x2d, g, w_in_all, w_out_all, layer, final_g, final_norm, name):
    t = x2d.shape[0]
    tm = ROW_TILE
    return pl.pallas_call(
        functools.partial(_ffn_kernel, final_norm=final_norm),
        grid=(t // tm,),
        in_specs=[pl.BlockSpec((mixed.shape[0], tm, mixed.shape[2]), lambda i: (0, i, 0))]
        + _ffn_specs(w_proj, layer),
        out_specs=pl.BlockSpec((tm, D_MODEL), lambda i: (i, 0)),
        out_shape=jax.ShapeDtypeStruct((t, D_MODEL), F32),
        compiler_params=_params("arbitrary"),
        name=name,
    )(mixed, w_proj, x2d, g, w_in_all, w_out_all, final_g)


def _ret_ffn(q, kt, v, sg, w_proj, x2d, g, w_in_all, w_out_all, layer, final_g, seq):
    t = x2d.shape[0]
    tm = ROW_TILE
    r = RET_BLOCK
    n = t // tm
    mask, qd, kd, block_decay = _retention_tables()
    retained = lambda i: (jnp.minimum(i, n - 1), 0)
    finished = lambda i: (jnp.maximum(i - 1, 0), 0)
    const3 = lambda i: (0, 0, 0)
    once = dict(pipeline_mode=pl.Buffered(1))
    return pl.pallas_call(
        functools.partial(_ret_ffn_kernel, tiles_per_seq=seq // tm, block_decay=block_decay),
        grid=(n + 1,),
        in_specs=[
            pl.BlockSpec((tm, D_MODEL), retained),
            pl.BlockSpec((None, D_MODEL, tm), lambda i: (jnp.minimum(i, n - 1), 0, 0)),
            pl.BlockSpec((tm, 2 * D_MODEL), retained),
            pl.BlockSpec((tm, 2 * D_MODEL), retained),
            pl.BlockSpec((RET_HEADS, r, r), const3, **once),
            pl.BlockSpec((RET_HEADS, r, RET_QK_DIM), const3, **once),
            pl.BlockSpec((RET_HEADS, 1, r), const3, **once),
        ]
        + _ffn_specs(w_proj, layer, finished),
        out_specs=pl.BlockSpec((tm, D_MODEL), finished),
        out_shape=jax.ShapeDtypeStruct((t, D_MODEL), F32),
        scratch_shapes=[
            pltpu.VMEM((tm, 2 * D_MODEL), BF16),
            pltpu.VMEM((RET_HEADS, RET_QK_DIM, RET_V_DIM), F32),
        ],
        compiler_params=_params("arbitrary"),
        name="ret_ffn",
    )(q, kt, v, sg, mask, qd, kd, w_proj, x2d, g, w_in_all, w_out_all, final_g)


def _ret_project_head(hd, rows, h, wq_ref, wk_ref, wv_ref, wg_ref, cos_ref, sin_ref, cost_ref, sint_ref,
                      q_ref, kt_ref, v_ref, sg_ref):
    half = RET_QK_DIM // 2
    c0 = hd * RET_QK_DIM
    cos, sin = cos_ref[rows, :], sin_ref[rows, :]
    acc = _dot(h, wq_ref[:, c0:c0 + RET_QK_DIM])
    t1, t2 = acc[:, :half], acc[:, half:]
    q_ref[rows, c0:c0 + half] = (t1 * cos - t2 * sin).astype(BF16)
    q_ref[rows, c0 + half:c0 + RET_QK_DIM] = (t1 * sin + t2 * cos).astype(BF16)
    cos_t, sin_t = cost_ref[:, rows], sint_ref[:, rows]
    k_scale = RET_QK_DIM ** -0.5
    acc = jnp.transpose(_dot(h, wk_ref[:, c0:c0 + RET_QK_DIM]))
    t1, t2 = acc[:half], acc[half:]
    kt_ref[c0:c0 + half, rows] = ((t1 * cos_t - t2 * sin_t) * k_scale).astype(BF16)
    kt_ref[c0 + half:c0 + RET_QK_DIM, rows] = ((t1 * sin_t + t2 * cos_t) * k_scale).astype(BF16)
    v0 = hd * RET_V_DIM
    gate = _dot(h, wg_ref[:, v0:v0 + RET_V_DIM])
    sg_ref[rows, v0:v0 + RET_V_DIM] = (gate * jax.nn.sigmoid(gate)).astype(BF16)
    v_ref[rows, v0:v0 + RET_V_DIM] = _dot(h, wv_ref[:, v0:v0 + RET_V_DIM]).astype(BF16)


def _retention_stages(q_ref, kt_ref, v_ref, sg_ref, w_ref, qd_ref, kd_ref, o_ref, state_ref, block_decay):
    r = RET_BLOCK
    stages = []
    for hd in range(RET_HEADS):
        qk_cols = slice(hd * RET_QK_DIM, (hd + 1) * RET_QK_DIM)
        v_cols = slice(hd * RET_V_DIM, (hd + 1) * RET_V_DIM)
        for blk in range(q_ref.shape[0] // r):
            rows = slice(blk * r, (blk + 1) * r)
            held = {}

            def score(hd=hd, rows=rows, qk_cols=qk_cols, held=held):
                q = q_ref[rows, qk_cols]
                kt = kt_ref[qk_cols, rows]
                held["p"] = (_dot(q, kt) * w_ref[hd]).astype(BF16)
                held["q_dec"] = (q.astype(F32) * qd_ref[hd]).astype(BF16)
                held["kt_dec"] = (kt.astype(F32) * kd_ref[hd]).astype(BF16)

            def mix(hd=hd, rows=rows, v_cols=v_cols, held=held):
                v = v_ref[rows, v_cols]
                state = state_ref[hd]
                held["y"] = _dot(jnp.concatenate([held["p"], held["q_dec"]], axis=1),
                                 jnp.concatenate([v, state.astype(BF16)], axis=0))
                state_ref[hd] = state * block_decay[hd] + _dot(held["kt_dec"], v)

            def norm(rows=rows, v_cols=v_cols, held=held):
                y = held["y"]
                mu = jnp.mean(y, axis=-1, keepdims=True)
                yc = y - mu
                var = jnp.mean(yc * yc, axis=-1, keepdims=True)
                normed = (yc * lax.rsqrt(var + EPS)).astype(BF16)
                o_ref[rows, v_cols] = sg_ref[rows, v_cols] * normed

            stages += [score, mix, norm]
    return stages


def _retention_tables():
    r = RET_BLOCK
    gamma = 1.0 - 2.0 ** (-5.0 - np.arange(RET_HEADS, dtype=np.float64))
    pos = np.arange(r)
    diff = (pos[:, None] - pos[None, :]).astype(np.float64)
    same = (pos[:, None] // CHUNK) == (pos[None, :] // CHUNK)
    earlier = (pos[None, :] // CHUNK) < (pos[:, None] // CHUNK)
    expo = np.where(same, np.abs(diff), diff)
    w = np.where((same | earlier)[None], gamma[:, None, None] ** expo[None], 0.0)
    idx = np.arange(r, dtype=np.float64)
    qd = gamma[:, None] ** (idx[None, :] + 1.0)
    qd = np.broadcast_to(qd[..., None], (RET_HEADS, r, RET_QK_DIM))
    kd = (gamma[:, None] ** (r - 1.0 - idx[None, :]))[:, None, :]
    block_decay = tuple(float(g) ** r for g in gamma)
    f32 = lambda a: np.ascontiguousarray(a, dtype=np.float32)
    return f32(w), f32(qd), f32(kd), block_decay


def _ret_proj_kernel(x_ref, g_ref, wq_ref, wk_ref, wv_ref, wg_ref, cos_ref, sin_ref, cost_ref, sint_ref,
                     q_ref, kt_ref, v_ref, sg_ref):
    tm = x_ref.shape[0]
    halves = (slice(0, tm // 2), slice(tm // 2, tm))
    hs = [_rmsnorm(x_ref[rows, :], g_ref[...]).astype(BF16) for rows in halves]
    for hd in range(RET_HEADS):
        for rows, h in zip(halves, hs):
            _ret_project_head(hd, rows, h, wq_ref, wk_ref, wv_ref, wg_ref, cos_ref, sin_ref, cost_ref,
                              sint_ref, q_ref, kt_ref, v_ref, sg_ref)


def _ret_proj(x2d, g, w, cos, sin, seq):
    t = x2d.shape[0]
    tm = ROW_TILE
    tiles_per_seq = seq // tm
    half = RET_QK_DIM // 2
    row = lambda i: (i, 0)
    const = lambda i: (0, 0)
    return pl.pallas_call(
        _ret_proj_kernel,
        grid=(t // tm,),
        in_specs=[
            pl.BlockSpec((tm, D_MODEL), row),
            pl.BlockSpec((1, D_MODEL), const),
            pl.BlockSpec((D_MODEL, D_MODEL), const, pipeline_mode=pl.Buffered(1)),
            pl.BlockSpec((D_MODEL, D_MODEL), lambda i: (0, 1), pipeline_mode=pl.Buffered(1)),
            pl.BlockSpec((D_MODEL, 2 * D_MODEL), lambda i: (0, 1), pipeline_mode=pl.Buffered(1)),
            pl.BlockSpec((D_MODEL, 2 * D_MODEL), lambda i: (0, 2), pipeline_mode=pl.Buffered(1)),
            pl.BlockSpec((tm, half), lambda i: (i % tiles_per_seq, 0)),
            pl.BlockSpec((tm, half), lambda i: (i % tiles_per_seq, 0)),
            pl.BlockSpec((half, tm), lambda i: (0, i % tiles_per_seq)),
            pl.BlockSpec((half, tm), lambda i: (0, i % tiles_per_seq)),
        ],
        out_specs=[
            pl.BlockSpec((tm, D_MODEL), row),
            pl.BlockSpec((None, D_MODEL, tm), lambda i: (i, 0, 0)),
            pl.BlockSpec((tm, 2 * D_MODEL), row),
            pl.BlockSpec((tm, 2 * D_MODEL), row),
        ],
        out_shape=[
            jax.ShapeDtypeStruct((t, D_MODEL), BF16),
            jax.ShapeDtypeStruct((t // tm, D_MODEL, tm), BF16),
            jax.ShapeDtypeStruct((t, 2 * D_MODEL), BF16),
            jax.ShapeDtypeStruct((t, 2 * D_MODEL), BF16),
        ],
        compiler_params=_params("arbitrary"),
        name="ret_proj",
    )(x2d, g, w, w, w, w, cos, sin, np.ascontiguousarray(cos.T), np.ascontiguousarray(sin.T))


def kernel(x, norm_mix, norm_ffn, fox_w_in, fox_b_f, fox_w_out, ret_w_in, ret_w_out,
           ffn_w_in, ffn_w_out, final_norm):
    batch, seq, d = x.shape
    t = batch * seq
    x2d = x.reshape(t, d)
    row = lambda a: a.reshape(1, -1)

    w_in = fox_w_in[0]
    w_qkv = w_in.astype(BF16)
    pad = LANES - C_PARTS * FOX_HEADS
    wf = jnp.pad(jnp.tile(w_in[:, 3 * D_MODEL:], (1, C_PARTS)), ((0, 0), (0, pad))).astype(BF16)
    bf = jnp.pad(jnp.tile(fox_b_f[0], C_PARTS), (0, pad)).reshape(1, LANES)
    k, qvt, ca = _fox_proj(x2d, row(norm_mix[0]), w_qkv, wf, bf, seq)
    attn = _fox_attn(k, ca, qvt, batch, seq)
    ffn_in = ffn_w_in.astype(BF16)
    ffn_out = ffn_w_out.astype(BF16)
    x2d = _ffn(attn, fox_w_out[0].astype(BF16), x2d, row(norm_ffn[0]), ffn_in, ffn_out, 0,
               row(final_norm), False, "ffn0")

    half = RET_QK_DIM // 2
    inv = ROPE_BASE ** (-np.arange(half, dtype=np.float64) / half)
    ang = np.arange(seq, dtype=np.float64)[:, None] * inv[None, :]
    cos, sin = np.cos(ang).astype(np.float32), np.sin(ang).astype(np.float32)
    w_in = ret_w_in[0]
    q, kt, v, sg = _ret_proj(x2d, row(norm_mix[1]), w_in.astype(BF16), cos, sin, seq)
    x2d = _ret_ffn(q, kt, v, sg, ret_w_out[0].astype(BF16), x2d, row(norm_ffn[1]), ffn_in, ffn_out, 1,
                   row(final_norm), seq)
    return x2d.reshape(batch, seq, d)
```
